```python
import jax, jax.numpy as jnp
from jax import lax
import numpy as np

D_MODEL = 2048
BATCH = 2
SEQ = 8192
DEPTH = 2

D_MIX = D_MODEL
ATT_HEAD_DIM = 128
D_ATT = D_MIX // 2
ATT_HEADS = D_ATT // ATT_HEAD_DIM
D_RNN = D_MIX // 4
RNN_HEADS = 4
RNN_HEAD_DIM = D_RNN // RNN_HEADS
POOL_WINDOWS = (2, 4, 8, 16)
D_POOL = D_MIX - D_ATT - D_RNN
POOL_GROUPS = len(POOL_WINDOWS)
POOL_GROUP_DIM = D_POOL // POOL_GROUPS
D_IN_PROJ = 3 * D_ATT + 2 * D_RNN + D_POOL
CONV_WIDTH = 4
RG_C = 8.0
D_FF = 5632
Q_BLOCK = 128
NORM_EPS = 1e-6

kernel_name = "hybrid_stickbreak_rglru_pool_macaron"


def rms_norm(x, g):
    xf = x.astype(jnp.float32)
    y = xf * lax.rsqrt(jnp.mean(xf * xf, axis=-1, keepdims=True) + NORM_EPS)
    return (y * g.astype(jnp.float32)).astype(x.dtype)


def swiglu(h, w_gate, w_up, w_down):
    return (jax.nn.silu(h @ w_gate) * (h @ w_up)) @ w_down


def stick_breaking_attention(q, k, v):
    B, S, H, dh = q.shape
    scale = dh ** -0.5
    outs = []
    for blk in range(S // Q_BLOCK):
        start, end = blk * Q_BLOCK, (blk + 1) * Q_BLOCK
        qb = q[:, start:end]
        kb, vb = k[:, :end], v[:, :end]
        z = jnp.einsum('bthd,bshd->bhts', qb, kb,
                       preferred_element_type=jnp.float32) * scale
        t_pos = start + jnp.arange(Q_BLOCK, dtype=jnp.int32)
        s_pos = jnp.arange(end, dtype=jnp.int32)
        causal = s_pos[None, :] < t_pos[:, None]
        log_fail = jnp.where(causal, jax.nn.log_sigmoid(-z), 0.0)
        after = lax.cumsum(log_fail, axis=3, reverse=True) - log_fail
        w = jnp.where(causal, jnp.exp(jax.nn.log_sigmoid(z) + after), 0.0)
        outs.append(jnp.einsum('bhts,bshd->bthd', w.astype(vb.dtype), vb))
    return jnp.concatenate(outs, axis=1)


def causal_depthwise_conv(x, w, b):
    C = x.shape[-1]
    y = lax.conv_general_dilated(
        x, w[:, None, :], window_strides=(1,), padding=[(CONV_WIDTH - 1, 0)],
        dimension_numbers=('NWC', 'WIO', 'NWC'), feature_group_count=C)
    return y + b


def _scan_combine(c1, c2):
    a1, b1 = c1
    a2, b2 = c2
    return a1 * a2, a2 * b1 + b2


def rg_lru_mixer(xg, xr, conv_w, conv_b, w_a, b_a, w_x, b_x, lam):
    B, S, _ = xr.shape
    u = causal_depthwise_conv(xr, conv_w, conv_b)
    uh = u.reshape(B, S, RNN_HEADS, RNN_HEAD_DIM)
    r = jax.nn.sigmoid(jnp.einsum('bshi,hij->bshj', uh, w_a).reshape(B, S, D_RNN) + b_a)
    i = jax.nn.sigmoid(jnp.einsum('bshi,hij->bshj', uh, w_x).reshape(B, S, D_RNN) + b_x)
    log_a = RG_C * r.astype(jnp.float32) * jax.nn.log_sigmoid(lam.astype(jnp.float32))
    a = jnp.exp(log_a)
    b = jnp.sqrt(-jnp.expm1(2.0 * log_a)) * (i * u).astype(jnp.float32)
    _, h = lax.associative_scan(_scan_combine, (a, b), axis=1)
    return jax.nn.gelu(xg) * h.astype(xg.dtype)


def pool_mixer(xp, w_pool, scale):
    B, S, _ = xp.shape
    xg = xp.reshape(B, S, POOL_GROUPS, POOL_GROUP_DIM).astype(jnp.float32)
    cs = jnp.cumsum(xg, axis=1)
    t = jnp.arange(S, dtype=jnp.int32)
    outs = []
    for g, win in enumerate(POOL_WINDOWS):
        c = cs[:, :, g]
        lower = jnp.pad(c, ((0, 0), (win, 0), (0, 0)))[:, :S]
        count = jnp.minimum(t + 1, win).astype(jnp.float32)[None, :, None]
        outs.append((c - lower) / count - xg[:, :, g])
    d = jnp.stack(outs, axis=2).astype(xp.dtype)
    y = jnp.einsum('bsgi,gij->bsgj', d, w_pool).reshape(B, S, D_POOL)
    return y * scale


def setup_inputs(seed: int = 0) -> dict:
    key = jax.random.key(seed)
    ks = jax.random.split(key, 24)
    f32 = jnp.float32

    def nrm(k, shape, fan_in):
        return jax.random.normal(k, shape, f32) * (fan_in ** -0.5)

    def gain(k, shape):
        return 1.0 + 0.02 * jax.random.normal(k, shape, f32)

    def bias(k, shape):
        return 0.02 * jax.random.normal(k, shape, f32)

    L = DEPTH
    out_scale = (2.0 * DEPTH) ** -0.5
    u = jax.random.uniform(ks[13], (L, D_RNN), f32, 0.9, 0.999)
    s = u ** (1.0 / RG_C)
    rg_lambda = jnp.log(s) - jnp.log1p(-s)
    return {
        "x": jax.random.normal(ks[0], (BATCH, SEQ, D_MODEL), f32),
        "norm_ffn1": gain(ks[1], (L, D_MODEL)),
        "ffn1_gate": nrm(ks[2], (L, D_MODEL, D_FF), D_MODEL),
        "ffn1_up": nrm(ks[3], (L, D_MODEL, D_FF), D_MODEL),
        "ffn1_down": nrm(ks[4], (L, D_FF, D_MODEL), D_FF) * out_scale,
        "norm_mix": gain(ks[5], (L, D_MODEL)),
        "w_in": nrm(ks[6], (L, D_MODEL, D_IN_PROJ), D_MODEL),
        "conv_w": nrm(ks[7], (L, CONV_WIDTH, D_RNN), CONV_WIDTH),
        "conv_b": bias(ks[8], (L, D_RNN)),
        "rg_w_a": nrm(ks[9], (L, RNN_HEADS, RNN_HEAD_DIM, RNN_HEAD_DIM), RNN_HEAD_DIM),
        "rg_b_a": bias(ks[10], (L, D_RNN)),
        "rg_w_x": nrm(ks[11], (L, RNN_HEADS, RNN_HEAD_DIM, RNN_HEAD_DIM), RNN_HEAD_DIM),
        "rg_b_x": bias(ks[12], (L, D_RNN)),
        "rg_lambda": rg_lambda,
        "pool_w": nrm(ks[14], (L, POOL_GROUPS, POOL_GROUP_DIM, POOL_GROUP_DIM), POOL_GROUP_DIM),
        "pool_scale": gain(ks[15], (L, D_POOL)),
        "w_out": nrm(ks[16], (L, D_MIX, D_MODEL), D_MIX) * out_scale,
        "norm_ffn2": gain(ks[17], (L, D_MODEL)),
        "ffn2_gate": nrm(ks[18], (L, D_MODEL, D_FF), D_MODEL),
        "ffn2_up": nrm(ks[19], (L, D_MODEL, D_FF), D_MODEL),
        "ffn2_down": nrm(ks[20], (L, D_FF, D_MODEL), D_FF) * out_scale,
        "norm_final": gain(ks[21], (D_MODEL,)),
    }


def reference(x, norm_ffn1, ffn1_gate, ffn1_up, ffn1_down, norm_mix, w_in, conv_w, conv_b,
              rg_w_a, rg_b_a, rg_w_x, rg_b_x, rg_lambda, pool_w, pool_scale, w_out,
              norm_ffn2, ffn2_gate, ffn2_up, ffn2_down, norm_final):
    B, S, _ = x.shape
    splits = [int(v) for v in np.cumsum([D_ATT, D_ATT, D_ATT, D_RNN, D_RNN])]
    for l in range(DEPTH):
        x = x + 0.5 * swiglu(rms_norm(x, norm_ffn1[l]), ffn1_gate[l], ffn1_up[l], ffn1_down[l])
        h = rms_norm(x, norm_mix[l])
        z = h @ w_in[l]
        q, k, v, rg_gate, rg_x, pool_in = jnp.split(z, splits, axis=-1)
        att = stick_breaking_attention(
            q.reshape(B, S, ATT_HEADS, ATT_HEAD_DIM),
            k.reshape(B, S, ATT_HEADS, ATT_HEAD_DIM),
            v.reshape(B, S, ATT_HEADS, ATT_HEAD_DIM)).reshape(B, S, D_ATT)
        rnn = rg_lru_mixer(rg_gate, rg_x, conv_w[l], conv_b[l], rg_w_a[l], rg_b_a[l],
                           rg_w_x[l], rg_b_x[l], rg_lambda[l])
        pool = pool_mixer(pool_in, pool_w[l], pool_scale[l])
        x = x + jnp.concatenate([att, rnn, pool], axis=-1) @ w_out[l]
        x = x + 0.5 * swiglu(rms_norm(x, norm_ffn2[l]), ffn2_gate[l], ffn2_up[l], ffn2_down[l])
    return rms_norm(x, norm_final)
```

```python
import functools
import math

import jax
import jax.numpy as jnp
from jax import lax
from jax.experimental import pallas as pl
from jax.experimental.pallas import tpu as pltpu

D_MODEL = 2048
D_ATT = 1024
ATT_HEADS = 8
HEAD_DIM = 128
D_RNN = 512
RNN_HEADS = 4
D_POOL = 512
POOL_WINDOWS = (2, 4, 8, 16)
D_QKV = 3 * D_ATT
D_REST = 2 * D_RNN + D_POOL
CONV_WIDTH = 4
RG_C = 8.0
D_FF = 5632
NORM_EPS = 1e-6

VMEM_LIMIT_BYTES = 56 * 1024 * 1024

FFN_TOKEN_TILE = 512
FFN_HIDDEN_TILE = 512
MIX_TOKEN_TILE = 512
ATT_BLOCK = 256
ATT_LOG_CUTOFF = -100.0
SEQ_TILE = 512
HISTORY_ROWS = 16

_bf16 = jnp.bfloat16
_f32 = jnp.float32


def _rms_norm(x, gain):
    ms = jnp.mean(x * x, axis=-1, keepdims=True)
    return x * lax.rsqrt(ms + NORM_EPS) * gain


def _log_sigmoid(x):
    return jnp.minimum(x, 0.0) - jnp.log1p(jnp.exp(-jnp.abs(x)))


def _sigmoid(x):
    return 1.0 / (1.0 + jnp.exp(-x))


def _ffn_kernel(x_ref, gain_ref, wg_ref, wu_ref, wd_ref, fgain_ref, o_ref, h_ref, acc_ref,
                *, final_norm):
    j = pl.program_id(1)

    @pl.when(j == 0)
    def _():
        h_ref[...] = _rms_norm(x_ref[...], gain_ref[...]).astype(_bf16)
        acc_ref[...] = jnp.zeros_like(acc_ref)

    h = h_ref[...]
    g = jnp.dot(h, wg_ref[...], preferred_element_type=_f32)
    u = jnp.dot(h, wu_ref[...], preferred_element_type=_f32)
    a = (g * _sigmoid(g) * u).astype(_bf16)
    acc_ref[...] += jnp.dot(a, wd_ref[...], preferred_element_type=_f32)

    @pl.when(j == pl.num_programs(1) - 1)
    def _():
        y = x_ref[...] + 0.5 * acc_ref[...]
        if final_norm:
            y = _rms_norm(y, fgain_ref[...])
        o_ref[...] = y


def _ffn(x, gain, wg, wu, wd, final_gain, final_norm):
    t = x.shape[0]
    tm, tf = FFN_TOKEN_TILE, FFN_HIDDEN_TILE
    return pl.pallas_call(
        functools.partial(_ffn_kernel, final_norm=final_norm),
        grid=(t // tm, D_FF // tf),
        in_specs=[
            pl.BlockSpec((tm, D_MODEL), lambda i, j: (i, 0)),
            pl.BlockSpec((1, D_MODEL), lambda i, j: (0, 0)),
            pl.BlockSpec((D_MODEL, tf), lambda i, j: (0, j)),
            pl.BlockSpec((D_MODEL, tf), lambda i, j: (0, j)),
            pl.BlockSpec((tf, D_MODEL), lambda i, j: (j, 0)),
            pl.BlockSpec((1, D_MODEL), lambda i, j: (0, 0)),
        ],
        out_specs=pl.BlockSpec((tm, D_MODEL), lambda i, j: (i, 0)),
        out_shape=jax.ShapeDtypeStruct((t, D_MODEL), _f32),
        scratch_shapes=[pltpu.VMEM((tm, D_MODEL), _bf16), pltpu.VMEM((tm, D_MODEL), _f32)],
        compiler_params=pltpu.CompilerParams(
            dimension_semantics=("parallel", "arbitrary"),
            vmem_limit_bytes=VMEM_LIMIT_BYTES),
        name="ffn",
    )(x, gain, wg, wu, wd, final_gain)


def _mix_in_kernel(x_ref, gain_ref, w_ref, qkv_ref, rest_ref):
    h = _rms_norm(x_ref[...], gain_ref[...]).astype(_bf16)
    qkv_ref[...] = jnp.dot(h, w_ref[:, :D_QKV], preferred_element_type=_f32).astype(_bf16)
    rest_ref[...] = jnp.dot(h, w_ref[:, D_QKV:], preferred_element_type=_f32)


def _mix_in(x, gain, w_in):
    t = x.shape[0]
    tm = MIX_TOKEN_TILE
    return pl.pallas_call(
        _mix_in_kernel,
        grid=(t // tm,),
        in_specs=[
            pl.BlockSpec((tm, D_MODEL), lambda i: (i, 0)),
            pl.BlockSpec((1, D_MODEL), lambda i: (0, 0)),
            pl.BlockSpec((D_MODEL, D_QKV + D_REST), lambda i: (0, 0),
                         pipeline_mode=pl.Buffered(1)),
        ],
        out_specs=[
            pl.BlockSpec((tm, D_QKV), lambda i: (i, 0)),
            pl.BlockSpec((tm, D_REST), lambda i: (i, 0)),
        ],
        out_shape=[jax.ShapeDtypeStruct((t, D_QKV), _bf16),
                   jax.ShapeDtypeStruct((t, D_REST), _f32)],
        compiler_params=pltpu.CompilerParams(
            dimension_semantics=("parallel",), vmem_limit_bytes=VMEM_LIMIT_BYTES),
        name="mix_in",
    )(x, gain, w_in)


def _attention_kernel(q_ref, k_ref, v_ref, o_ref, acc_ref, carry_ref):
    i = pl.program_id(2)
    blk = ATT_BLOCK
    q = q_ref[...]
    scale = HEAD_DIM ** -0.5
    row = lax.broadcasted_iota(jnp.int32, (blk, blk), 0)
    col = lax.broadcasted_iota(jnp.int32, (blk, blk), 1)
    upper = (row > col).astype(_bf16)
    causal = col < row

    acc_ref[...] = jnp.zeros_like(acc_ref)
    carry_ref[...] = jnp.zeros_like(carry_ref)

    def visit(jb, diagonal):
        start = pl.multiple_of(jb * blk, blk)
        kb = k_ref[pl.ds(start, blk), :]
        vb = v_ref[pl.ds(start, blk), :]
        z = lax.dot_general(q, kb, (((1,), (1,)), ((), ())),
                            preferred_element_type=_f32) * scale
        log_fail = _log_sigmoid(-z)
        if diagonal:
            log_fail = jnp.where(causal, log_fail, 0.0)
        hi = log_fail.astype(_bf16)
        lo = (log_fail - hi.astype(_f32)).astype(_bf16)
        after_in = (jnp.dot(hi, upper, preferred_element_type=_f32)
                    + jnp.dot(lo, upper, preferred_element_type=_f32))
        carry = carry_ref[...]
        w = jnp.exp(z + log_fail + after_in + carry)
        if diagonal:
            w = jnp.where(causal, w, 0.0)
        acc_ref[...] += jnp.dot(w.astype(_bf16), vb, preferred_element_type=_f32)
        carry_ref[...] = carry + after_in[:, 0:1] + log_fail[:, 0:1]

    visit(i, True)

    def cond(jb):
        return jnp.logical_and(jb >= 0, jnp.max(carry_ref[...]) > ATT_LOG_CUTOFF)

    def body(jb):
        visit(jb, False)
        return jb - 1

    lax.while_loop(cond, body, i - 1)
    o_ref[...] = acc_ref[...].astype(o_ref.dtype)


def _attention(qkv, batch, seq):
    blk = ATT_BLOCK
    qkv3 = qkv.reshape(batch, seq, D_QKV)
    out = pl.pallas_call(
        _attention_kernel,
        grid=(batch, ATT_HEADS, seq // blk),
        in_specs=[
            pl.BlockSpec((None, blk, HEAD_DIM), lambda b, h, i: (b, i, h)),
            pl.BlockSpec((None, seq, HEAD_DIM), lambda b, h, i: (b, 0, ATT_HEADS + h)),
            pl.BlockSpec((None, seq, HEAD_DIM), lambda b, h, i: (b, 0, 2 * ATT_HEADS + h)),
        ],
        out_specs=pl.BlockSpec((None, blk, HEAD_DIM), lambda b, h, i: (b, i, h)),
        out_shape=jax.ShapeDtypeStruct((batch, seq, D_ATT), _bf16),
        scratch_shapes=[pltpu.VMEM((blk, HEAD_DIM), _f32), pltpu.VMEM((blk, 1), _f32)],
        compiler_params=pltpu.CompilerParams(
            dimension_semantics=("parallel", "parallel", "arbitrary"),
            vmem_limit_bytes=VMEM_LIMIT_BYTES),
        name="attention",
    )(qkv3, qkv3, qkv3)
    return out.reshape(batch * seq, D_ATT)


def _rnn_pool_kernel(rest_ref, conv_w_ref, conv_b_ref, wa_ref, ba_ref, wx_ref, bx_ref,
                     lam_ref, pool_w_ref, pool_scale_ref, o_ref,
                     xr_ext, xp_ext, h_ref):
    s = pl.program_id(1)
    tc = SEQ_TILE
    hist = HISTORY_ROWS

    @pl.when(s == 0)
    def _():
        xr_ext[0:hist, :] = jnp.zeros((hist, D_RNN), _f32)
        xp_ext[0:hist, :] = jnp.zeros((hist, D_POOL), _f32)
        h_ref[...] = jnp.zeros_like(h_ref)

    xg = rest_ref[:, 0:D_RNN]
    xr_ext[hist:, :] = rest_ref[:, D_RNN:2 * D_RNN]
    xp_ext[hist:, :] = rest_ref[:, 2 * D_RNN:]

    u = jnp.zeros((tc, D_RNN), _f32) + conv_b_ref[...]
    for j in range(CONV_WIDTH):
        off = hist - (CONV_WIDTH - 1) + j
        u = u + conv_w_ref[j:j + 1, :] * xr_ext[off:off + tc, :]

    ub = u.astype(_bf16)
    r_parts, i_parts = [], []
    for hd in range(RNN_HEADS):
        uh = ub[:, hd * HEAD_DIM:(hd + 1) * HEAD_DIM]
        r_parts.append(jnp.dot(uh, wa_ref[hd], preferred_element_type=_f32))
        i_parts.append(jnp.dot(uh, wx_ref[hd], preferred_element_type=_f32))
    r = _sigmoid(jnp.concatenate(r_parts, axis=1) + ba_ref[...])
    ig = _sigmoid(jnp.concatenate(i_parts, axis=1) + bx_ref[...])
    log_a = RG_C * r * _log_sigmoid(lam_ref[...])
    a = jnp.exp(log_a)
    b = jnp.sqrt(-jnp.tanh(log_a) * (1.0 + a * a)) * (ig * u)

    t_idx = lax.broadcasted_iota(jnp.int32, (tc, D_RNN), 0)
    d = 1
    while d < tc:
        keep = t_idx >= d
        a_sh = jnp.where(keep, pltpu.roll(a, d, 0), 1.0)
        b_sh = jnp.where(keep, pltpu.roll(b, d, 0), 0.0)
        b = a * b_sh + b
        a = a * a_sh
        d *= 2
    h = a * h_ref[...] + b
    h_ref[...] = h[tc - 1:tc, :]

    c0 = math.sqrt(2.0 / math.pi)
    gelu = 0.5 * xg * (1.0 + jnp.tanh(c0 * (xg + 0.044715 * (xg * xg * xg))))
    o_ref[:, 0:D_RNN] = (gelu * h).astype(o_ref.dtype)

    pos = s * tc + lax.broadcasted_iota(jnp.int32, (tc, HEAD_DIM), 0)
    for g, win in enumerate(POOL_WINDOWS):
        lo, hi = g * HEAD_DIM, (g + 1) * HEAD_DIM
        cur = xp_ext[hist:, lo:hi]
        tot = cur
        for j in range(1, win):
            tot = tot + xp_ext[hist - j:hist - j + tc, lo:hi]
        count = jnp.minimum(pos + 1, win).astype(_f32)
        dlt = (tot / count - cur).astype(_bf16)
        y = jnp.dot(dlt, pool_w_ref[g], preferred_element_type=_f32)
        o_ref[:, D_RNN + lo:D_RNN + hi] = (y * pool_scale_ref[:, lo:hi]).astype(o_ref.dtype)

    xr_ext[0:hist, :] = xr_ext[tc:tc + hist, :]
    xp_ext[0:hist, :] = xp_ext[tc:tc + hist, :]


def _rnn_pool(rest, conv_w, conv_b, w_a, b_a, w_x, b_x, lam, pool_w, pool_scale, batch, seq):
    tc = SEQ_TILE
    rest3 = rest.reshape(batch, seq, D_REST)
    vec = lambda n: pl.BlockSpec((1, n), lambda b, s: (0, 0))
    mats = pl.BlockSpec((RNN_HEADS, HEAD_DIM, HEAD_DIM), lambda b, s: (0, 0, 0))
    out = pl.pallas_call(
        _rnn_pool_kernel,
        grid=(batch, seq // tc),
        in_specs=[
            pl.BlockSpec((None, tc, D_REST), lambda b, s: (b, s, 0)),
            pl.BlockSpec((CONV_WIDTH, D_RNN), lambda b, s: (0, 0)),
            vec(D_RNN), mats, vec(D_RNN), mats, vec(D_RNN), vec(D_RNN),
            mats, vec(D_POOL),
        ],
        out_specs=pl.BlockSpec((None, tc, D_RNN + D_POOL), lambda b, s: (b, s, 0)),
        out_shape=jax.ShapeDtypeStruct((batch, seq, D_RNN + D_POOL), _bf16),
        scratch_shapes=[
            pltpu.VMEM((HISTORY_ROWS + tc, D_RNN), _f32),
            pltpu.VMEM((HISTORY_ROWS + tc, D_POOL), _f32),
            pltpu.VMEM((1, D_RNN), _f32),
        ],
        compiler_params=pltpu.CompilerParams(
            dimension_semantics=("parallel", "arbitrary"),
            vmem_limit_bytes=VMEM_LIMIT_BYTES),
        name="rnn_pool",
    )(rest3, conv_w, conv_b, w_a, b_a, w_x, b_x, lam, pool_w, pool_scale)
    return out.reshape(batch * seq, D_RNN + D_POOL)


def _mix_out_kernel(x_ref, att_ref, rp_ref, w_ref, o_ref):
    y = jnp.dot(att_ref[...], w_ref[:D_ATT, :], preferred_element_type=_f32)
    y = y + jnp.dot(rp_ref[...], w_ref[D_ATT:, :], preferred_element_type=_f32)
    o_ref[...] = x_ref[...] + y


def _mix_out(x, att, rp, w_out):
    t = x.shape[0]
    tm = MIX_TOKEN_TILE
    return pl.pallas_call(
        _mix_out_kernel,
        grid=(t // tm,),
        in_specs=[
            pl.BlockSpec((tm, D_MODEL), lambda i: (i, 0)),
            pl.BlockSpec((tm, D_ATT), lambda i: (i, 0)),
            pl.BlockSpec((tm, D_RNN + D_POOL), lambda i: (i, 0)),
            pl.BlockSpec((D_MODEL, D_MODEL), lambda i: (0, 0), pipeline_mode=pl.Buffered(1)),
        ],
        out_specs=pl.BlockSpec((tm, D_MODEL), lambda i: (i, 0)),
        out_shape=jax.ShapeDtypeStruct((t, D_MODEL), _f32),
        compiler_params=pltpu.CompilerParams(
            dimension_semantics=("parallel",), vmem_limit_bytes=VMEM_LIMIT_BYTES),
        name="mix_out",
    )(x, att, rp, w_out)


def kernel(x, norm_ffn1, ffn1_gate, ffn1_up, ffn1_down, norm_mix, w_in, conv_w, conv_b, rg_w_a, rg_b_a, rg_w_x, rg_b_x, rg_lambda, pool_w, pool_scale, w_out, norm_ffn2, ffn2_gate, ffn2_up, ffn2_down, norm_final):
    batch, seq, _ = x.shape
    depth = w_in.shape[0]
    assert x.shape == (batch, seq, D_MODEL) and seq % max(ATT_BLOCK, SEQ_TILE) == 0
    assert (batch * seq) % max(FFN_TOKEN_TILE, MIX_TOKEN_TILE) == 0
    xf = x.reshape(batch * seq, D_MODEL)
    row = lambda v: v.reshape(1, -1)
    final_gain = row(norm_final)
    for l in range(depth):
        xf = _ffn(xf, row(norm_ffn1[l]), ffn1_gate[l].astype(_bf16), ffn1_up[l].astype(_bf16),
                  ffn1_down[l].astype(_bf16), final_gain, False)
        qkv, rest = _mix_in(xf, row(norm_mix[l]), w_in[l].astype(_bf16))
        att = _attention(qkv, batch, seq)
        rp = _rnn_pool(rest, conv_w[l], row(conv_b[l]), rg_w_a[l].astype(_bf16), row(rg_b_a[l]),
                       rg_w_x[l].astype(_bf16), row(rg_b_x[l]), row(rg_lambda[l]),
                       pool_w[l].astype(_bf16), row(pool_scale[l]), batch, seq)
        xf = _mix_out(xf, att, rp, w_out[l].astype(_bf16))
        xf = _ffn(xf, row(norm_ffn2[l]), ffn2_gate[l].astype(_bf16), ffn2_up[l].astype(_bf16),
                  ffn2_down[l].astype(_bf16), final_gain, l == depth - 1)
    return xf.reshape(batch, seq, D_MODEL)
```

```python
import functools
import math

import jax
import jax.numpy as jnp
from jax import lax
from jax.experimental import pallas as pl
from jax.experimental.pallas import tpu as pltpu

D_MODEL = 2048
D_ATT = 1024
ATT_HEADS = 8
HEAD_DIM = 128
D_RNN = 512
RNN_HEADS = 4
D_POOL = 512
POOL_WINDOWS = (2, 4, 8, 16)
D_QKV = 3 * D_ATT
D_REST = 2 * D_RNN + D_POOL
D_IN_PROJ = D_QKV + D_REST
CONV_WIDTH = 4
RG_C = 8.0
D_FF = 5632
NORM_EPS = 1e-6

VMEM_LIMIT_BYTES = 56 * 1024 * 1024

FFN_TOKEN_TILE = 1024
FFN_HIDDEN_TILE = 512
MIX_TOKEN_TILE = 512
ATT_BLOCK = 256
ATT_HEADS_PER_STEP = 4
ATT_LOG2_CUTOFF = -100.0 * math.log2(math.e)
MASKED_LOGIT = 1e30
SEQ_TILE = 512
HISTORY_ROWS = 16
DOWN_CAST_ROWS = 1408
W_IN_CAST_ROWS = 512
W_OUT_CAST_ROWS = 1024

_bf16 = jnp.bfloat16
_f32 = jnp.float32


def _rms_norm(x, gain):
    ms = jnp.mean(x * x, axis=-1, keepdims=True)
    return x * lax.rsqrt(ms + NORM_EPS) * gain


def _log_sigmoid(x):
    return jnp.minimum(x, 0.0) - jnp.log1p(jnp.exp(-jnp.abs(x)))


def _sigmoid(x):
    return 1.0 / (1.0 + jnp.exp(-x))


def _cast_kernel(w_ref, o_ref):
    o_ref[...] = w_ref[...].astype(_bf16)


def _cast_bf16(w, block_rows, name):
    depth, rows, cols = w.shape
    return pl.pallas_call(
        _cast_kernel,
        grid=(depth, rows // block_rows),
        in_specs=[pl.BlockSpec((None, block_rows, cols), lambda l, r: (l, r, 0))],
        out_specs=pl.BlockSpec((None, block_rows, cols), lambda l, r: (l, r, 0)),
        out_shape=jax.ShapeDtypeStruct(w.shape, _bf16),
        compiler_params=pltpu.CompilerParams(
            dimension_semantics=("parallel", "parallel"), vmem_limit_bytes=VMEM_LIMIT_BYTES),
        name=name,
    )(w)


def _merge_gate_up_kernel(g_ref, u_ref, o_ref):
    tf = g_ref.shape[-1]
    o_ref[:, :tf] = g_ref[...].astype(_bf16)
    o_ref[:, tf:] = u_ref[...].astype(_bf16)


def _merge_gate_up(gate, up):
    depth = gate.shape[0]
    tf = FFN_HIDDEN_TILE
    spec = pl.BlockSpec((None, D_MODEL, tf), lambda l, j: (l, 0, j))
    return pl.pallas_call(
        _merge_gate_up_kernel,
        grid=(depth, D_FF // tf),
        in_specs=[spec, spec],
        out_specs=pl.BlockSpec((None, D_MODEL, 2 * tf), lambda l, j: (l, 0, j)),
        out_shape=jax.ShapeDtypeStruct((depth, D_MODEL, 2 * D_FF), _bf16),
        compiler_params=pltpu.CompilerParams(
            dimension_semantics=("parallel", "parallel"), vmem_limit_bytes=VMEM_LIMIT_BYTES),
        name="merge_gate_up",
    )(gate, up)


def _ffn_kernel(x_ref, gain_ref, wgu_ref, wd_ref, fgain_ref, o_ref, h_ref, *, final_norm):
    j = pl.program_id(1)
    tf = FFN_HIDDEN_TILE

    @pl.when(j == 0)
    def _():
        h_ref[...] = _rms_norm(x_ref[...], gain_ref[...]).astype(_bf16)
        o_ref[...] = jnp.zeros_like(o_ref)

    gu = jnp.dot(h_ref[...], wgu_ref[...], preferred_element_type=_f32)
    g = gu[:, :tf]
    u = gu[:, tf:]
    a = (g * _sigmoid(g) * u).astype(_bf16)
    o_ref[...] += jnp.dot(a, wd_ref[...], preferred_element_type=_f32)

    @pl.when(j == pl.num_programs(1) - 1)
    def _():
        y = x_ref[...] + 0.5 * o_ref[...]
        if final_norm:
            y = _rms_norm(y, fgain_ref[...])
        o_ref[...] = y


def _ffn(x, gain, wgu, wd, layer, final_gain, final_norm):
    t = x.shape[0]
    tm, tf = FFN_TOKEN_TILE, FFN_HIDDEN_TILE
    return pl.pallas_call(
        functools.partial(_ffn_kernel, final_norm=final_norm),
        grid=(t // tm, D_FF // tf),
        in_specs=[
            pl.BlockSpec((tm, D_MODEL), lambda i, j: (i, 0), pipeline_mode=pl.Buffered(1)),
            pl.BlockSpec((1, D_MODEL), lambda i, j: (0, 0)),
            pl.BlockSpec((None, D_MODEL, 2 * tf), lambda i, j: (layer, 0, j)),
            pl.BlockSpec((None, tf, D_MODEL), lambda i, j: (layer, j, 0)),
            pl.BlockSpec((1, D_MODEL), lambda i, j: (0, 0)),
        ],
        out_specs=pl.BlockSpec((tm, D_MODEL), lambda i, j: (i, 0)),
        out_shape=jax.ShapeDtypeStruct((t, D_MODEL), _f32),
        scratch_shapes=[pltpu.VMEM((tm, D_MODEL), _bf16)],
        compiler_params=pltpu.CompilerParams(
            dimension_semantics=("parallel", "arbitrary"),
            vmem_limit_bytes=VMEM_LIMIT_BYTES),
        name="ffn",
    )(x, gain, wgu, wd, final_gain)


def _mix_in_kernel(x_ref, gain_ref, w_ref, qkv_ref, rest_ref):
    h = _rms_norm(x_ref[...], gain_ref[...]).astype(_bf16)
    qkv_ref[...] = jnp.dot(h, w_ref[:, :D_QKV], preferred_element_type=_f32).astype(_bf16)
    rest_ref[...] = jnp.dot(h, w_ref[:, D_QKV:], preferred_element_type=_f32)


def _mix_in(x, gain, w_in, layer):
    t = x.shape[0]
    tm = MIX_TOKEN_TILE
    return pl.pallas_call(
        _mix_in_kernel,
        grid=(t // tm,),
        in_specs=[
            pl.BlockSpec((tm, D_MODEL), lambda i: (i, 0)),
            pl.BlockSpec((1, D_MODEL), lambda i: (0, 0)),
            pl.BlockSpec((None, D_MODEL, D_IN_PROJ), lambda i: (layer, 0, 0),
                         pipeline_mode=pl.Buffered(1)),
        ],
        out_specs=[
            pl.BlockSpec((tm, D_QKV), lambda i: (i, 0)),
            pl.BlockSpec((tm, D_REST), lambda i: (i, 0)),
        ],
        out_shape=[jax.ShapeDtypeStruct((t, D_QKV), _bf16),
                   jax.ShapeDtypeStruct((t, D_REST), _f32)],
        compiler_params=pltpu.CompilerParams(
            dimension_semantics=("parallel",), vmem_limit_bytes=VMEM_LIMIT_BYTES),
        name="mix_in",
    )(x, gain, w_in)


def _log2_fail(y):
    return -(jnp.maximum(y, 0.0) + jnp.log2(1.0 + jnp.exp2(-jnp.abs(y))))


def _split_bf16(x):
    hi = x.astype(_bf16)
    lo = (x - hi.astype(_f32)).astype(_bf16)
    return hi, lo


def _attention_kernel(q_ref, k_ref, v_ref, o_ref, acc_ref, carry_ref):
    i = pl.program_id(2)
    blk = ATT_BLOCK
    logit_scale = HEAD_DIM ** -0.5 * math.log2(math.e)
    row = lax.broadcasted_iota(jnp.int32, (blk, blk), 0)
    col = lax.broadcasted_iota(jnp.int32, (blk, blk), 1)
    upper = (row > col).astype(_bf16)
    contract_last = (((1,), (1,)), ((), ()))

    first = jnp.maximum(i - 1, 0)
    start = pl.multiple_of(first * blk, blk)
    row2 = lax.broadcasted_iota(jnp.int32, (blk, 2 * blk), 0)
    col2 = lax.broadcasted_iota(jnp.int32, (blk, 2 * blk), 1)
    causal = col2 - row2 < (i - first) * blk

    heads = range(ATT_HEADS_PER_STEP)
    head_lanes = [slice(hd * HEAD_DIM, (hd + 1) * HEAD_DIM) for hd in heads]
    ys, lfs, css = [], [], []
    for lanes in head_lanes:
        k2 = k_ref[pl.ds(start, 2 * blk), lanes]
        y = lax.dot_general(q_ref[:, lanes], k2, contract_last,
                            preferred_element_type=_f32) * logit_scale
        ys.append(jnp.where(causal, y, -MASKED_LOGIT))
    for y in ys:
        lf = _log2_fail(y)
        hi, lo = _split_bf16(lf)
        stacked = jnp.concatenate([hi[:, blk:], lo[:, blk:], hi[:, :blk], lo[:, :blk]], axis=0)
        lfs.append(lf)
        css.append(jnp.dot(stacked, upper, preferred_element_type=_f32))
    for hd, lanes in zip(heads, head_lanes):
        y, lf, cs = ys[hd], lfs[hd], css[hd]
        after_r = cs[0:blk] + cs[blk:2 * blk]
        total_r = after_r[:, 0:1] + lf[:, blk:blk + 1]
        after_l = cs[2 * blk:3 * blk] + cs[3 * blk:] + total_r
        after = jnp.concatenate([after_l, after_r], axis=1)
        w = jnp.exp2(y + lf + after)
        v2 = v_ref[pl.ds(start, 2 * blk), lanes]
        acc_ref[hd] = jnp.dot(w.astype(_bf16), v2, preferred_element_type=_f32)
        carry_ref[hd] = after_l[:, 0:1] + lf[:, 0:1]

    def walk_earlier_blocks(hd, lanes):
        def cond(jb):
            return jnp.logical_and(jb >= 0, jnp.max(carry_ref[hd]) > ATT_LOG2_CUTOFF)

        def body(jb):
            kstart = pl.multiple_of(jb * blk, blk)
            kb = k_ref[pl.ds(kstart, blk), lanes]
            vb = v_ref[pl.ds(kstart, blk), lanes]
            yb = lax.dot_general(q_ref[:, lanes], kb, contract_last,
                                 preferred_element_type=_f32) * logit_scale
            lfb = _log2_fail(yb)
            hb, lb = _split_bf16(lfb)
            csb = jnp.dot(jnp.concatenate([hb, lb], axis=0), upper,
                          preferred_element_type=_f32)
            after_b = csb[0:blk] + csb[blk:] + carry_ref[hd]
            wb = jnp.exp2(yb + lfb + after_b)
            acc_ref[hd] += jnp.dot(wb.astype(_bf16), vb, preferred_element_type=_f32)
            carry_ref[hd] = after_b[:, 0:1] + lfb[:, 0:1]
            return jb - 1

        lax.while_loop(cond, body, first - 1)

    @pl.when(jnp.logical_and(first >= 1, jnp.max(carry_ref[...]) > ATT_LOG2_CUTOFF))
    def _():
        for hd, lanes in zip(heads, head_lanes):
            walk_earlier_blocks(hd, lanes)

    for hd, lanes in zip(heads, head_lanes):
        o_ref[:, lanes] = acc_ref[hd].astype(o_ref.dtype)


def _attention(qkv, batch, seq):
    blk = ATT_BLOCK
    hps = ATT_HEADS_PER_STEP
    groups = ATT_HEADS // hps
    width = hps * HEAD_DIM
    qkv3 = qkv.reshape(batch, seq, D_QKV)
    out = pl.pallas_call(
        _attention_kernel,
        grid=(batch, groups, seq // blk),
        in_specs=[
            pl.BlockSpec((None, blk, width), lambda b, g, i: (b, i, g)),
            pl.BlockSpec((None, seq, width), lambda b, g, i: (b, 0, groups + g)),
            pl.BlockSpec((None, seq, width), lambda b, g, i: (b, 0, 2 * groups + g)),
        ],
        out_specs=pl.BlockSpec((None, blk, width), lambda b, g, i: (b, i, g)),
        out_shape=jax.ShapeDtypeStruct((batch, seq, D_ATT), _bf16),
        scratch_shapes=[pltpu.VMEM((hps, blk, HEAD_DIM), _f32),
                        pltpu.VMEM((hps, blk, 1), _f32)],
        compiler_params=pltpu.CompilerParams(
            dimension_semantics=("parallel", "parallel", "arbitrary"),
            vmem_limit_bytes=VMEM_LIMIT_BYTES),
        name="attention",
    )(qkv3, qkv3, qkv3)
    return out.reshape(batch * seq, D_ATT)


def _rnn_pool_kernel(rest_ref, conv_w_ref, conv_b_ref, wa_ref, ba_ref, wx_ref, bx_ref,
                     lam_ref, pool_w_ref, pool_scale_ref, o_ref,
                     xr_ext, xp_ext, h_ref):
    s = pl.program_id(1)
    tc = SEQ_TILE
    hist = HISTORY_ROWS

    @pl.when(s == 0)
    def _():
        xr_ext[0:hist, :] = jnp.zeros((hist, D_RNN), _f32)
        xp_ext[0:hist, :] = jnp.zeros((hist, D_POOL), _f32)
        h_ref[...] = jnp.zeros_like(h_ref)

    xg = rest_ref[:, 0:D_RNN]
    xr_ext[hist:, :] = rest_ref[:, D_RNN:2 * D_RNN]
    xp_ext[hist:, :] = rest_ref[:, 2 * D_RNN:]

    u = jnp.zeros((tc, D_RNN), _f32) + conv_b_ref[...]
    for j in range(CONV_WIDTH):
        off = hist - (CONV_WIDTH - 1) + j
        u = u + conv_w_ref[j:j + 1, :] * xr_ext[off:off + tc, :]

    ub = u.astype(_bf16)
    r_parts, i_parts = [], []
    for hd in range(RNN_HEADS):
        uh = ub[:, hd * HEAD_DIM:(hd + 1) * HEAD_DIM]
        r_parts.append(jnp.dot(uh, wa_ref[hd], preferred_element_type=_f32))
        i_parts.append(jnp.dot(uh, wx_ref[hd], preferred_element_type=_f32))
    r = _sigmoid(jnp.concatenate(r_parts, axis=1) + ba_ref[...])
    ig = _sigmoid(jnp.concatenate(i_parts, axis=1) + bx_ref[...])
    log_a = RG_C * r * _log_sigmoid(lam_ref[...])
    a = jnp.exp(log_a)
    var = -jnp.tanh(log_a) * (1.0 + a * a)
    b = jnp.where(var > 0.0, var * lax.rsqrt(var), 0.0) * (ig * u)

    t_idx = lax.broadcasted_iota(jnp.int32, (tc, D_RNN), 0)
    d = 1
    while d < tc:
        keep = t_idx >= d
        a_sh = jnp.where(keep, pltpu.roll(a, d, 0), 1.0)
        b_sh = jnp.where(keep, pltpu.roll(b, d, 0), 0.0)
        b = a * b_sh + b
        a = a * a_sh
        d *= 2
    h = a * h_ref[...] + b
    h_ref[...] = h[tc - 1:tc, :]

    c0 = math.sqrt(2.0 / math.pi)
    gelu = 0.5 * xg * (1.0 + jnp.tanh(c0 * (xg + 0.044715 * (xg * xg * xg))))
    o_ref[:, 0:D_RNN] = (gelu * h).astype(o_ref.dtype)

    pos = s * tc + lax.broadcasted_iota(jnp.int32, (tc, HEAD_DIM), 0)
    for g, win in enumerate(POOL_WINDOWS):
        lo, hi = g * HEAD_DIM, (g + 1) * HEAD_DIM
        tot = xp_ext[:, lo:hi]
        span = 1
        while span < win:
            tot = tot + pltpu.roll(tot, span, 0)
            span *= 2
        cur = xp_ext[hist:, lo:hi]
        count = jnp.minimum(pos + 1, win).astype(_f32)
        dlt = (tot[hist:, :] / count - cur).astype(_bf16)
        y = jnp.dot(dlt, pool_w_ref[g], preferred_element_type=_f32)
        o_ref[:, D_RNN + lo:D_RNN + hi] = (y * pool_scale_ref[:, lo:hi]).astype(o_ref.dtype)

    xr_ext[0:hist, :] = xr_ext[tc:tc + hist, :]
    xp_ext[0:hist, :] = xp_ext[tc:tc + hist, :]


def _rnn_pool(rest, conv_w, conv_b, w_a, b_a, w_x, b_x, lam, pool_w, pool_scale, batch, seq):
    tc = SEQ_TILE
    rest3 = rest.reshape(batch, seq, D_REST)
    vec = lambda n: pl.BlockSpec((1, n), lambda b, s: (0, 0))
    mats = pl.BlockSpec((RNN_HEADS, HEAD_DIM, HEAD_DIM), lambda b, s: (0, 0, 0))
    out = pl.pallas_call(
        _rnn_pool_kernel,
        grid=(batch, seq // tc),
        in_specs=[
            pl.BlockSpec((None, tc, D_REST), lambda b, s: (b, s, 0)),
            pl.BlockSpec((CONV_WIDTH, D_RNN), lambda b, s: (0, 0)),
            vec(D_RNN), mats, vec(D_RNN), mats, vec(D_RNN), vec(D_RNN),
            mats, vec(D_POOL),
        ],
        out_specs=pl.BlockSpec((None, tc, D_RNN + D_POOL), lambda b, s: (b, s, 0)),
        out_shape=jax.ShapeDtypeStruct((batch, seq, D_RNN + D_POOL), _bf16),
        scratch_shapes=[
            pltpu.VMEM((HISTORY_ROWS + tc, D_RNN), _f32),
            pltpu.VMEM((HISTORY_ROWS + tc, D_POOL), _f32),
            pltpu.VMEM((1, D_RNN), _f32),
        ],
        compiler_params=pltpu.CompilerParams(
            dimension_semantics=("parallel", "arbitrary"),
            vmem_limit_bytes=VMEM_LIMIT_BYTES),
        name="rnn_pool",
    )(rest3, conv_w, conv_b, w_a, b_a, w_x, b_x, lam, pool_w, pool_scale)
    return out.reshape(batch * seq, D_RNN + D_POOL)


def _mix_out_kernel(x_ref, att_ref, rp_ref, w_ref, o_ref):
    y = jnp.dot(att_ref[...], w_ref[:D_ATT, :], preferred_element_type=_f32)
    y = y + jnp.dot(rp_ref[...], w_ref[D_ATT:, :], preferred_element_type=_f32)
    o_ref[...] = x_ref[...] + y


def _mix_out(x, att, rp, w_out, layer):
    t = x.shape[0]
    tm = MIX_TOKEN_TILE
    return pl.pallas_call(
        _mix_out_kernel,
        grid=(t // tm,),
        in_specs=[
            pl.BlockSpec((tm, D_MODEL), lambda i: (i, 0)),
            pl.BlockSpec((tm, D_ATT), lambda i: (i, 0)),
            pl.BlockSpec((tm, D_RNN + D_POOL), lambda i: (i, 0)),
            pl.BlockSpec((None, D_MODEL, D_MODEL), lambda i: (layer, 0, 0),
                         pipeline_mode=pl.Buffered(1)),
        ],
        out_specs=pl.BlockSpec((tm, D_MODEL), lambda i: (i, 0)),
        out_shape=jax.ShapeDtypeStruct((t, D_MODEL), _f32),
        compiler_params=pltpu.CompilerParams(
            dimension_semantics=("parallel",), vmem_limit_bytes=VMEM_LIMIT_BYTES),
        name="mix_out",
    )(x, att, rp, w_out)


def kernel(x, norm_ffn1, ffn1_gate, ffn1_up, ffn1_down, norm_mix, w_in, conv_w, conv_b, rg_w_a, rg_b_a, rg_w_x, rg_b_x, rg_lambda, pool_w, pool_scale, w_out, norm_ffn2, ffn2_gate, ffn2_up, ffn2_down, norm_final):
    batch, seq, _ = x.shape
    depth = w_in.shape[0]
    assert x.shape == (batch, seq, D_MODEL) and seq % max(2 * ATT_BLOCK, SEQ_TILE) == 0
    assert (batch * seq) % max(FFN_TOKEN_TILE, MIX_TOKEN_TILE) == 0
    xf = x.reshape(batch * seq, D_MODEL)
    row = lambda v: v.reshape(1, -1)
    final_gain = row(norm_final)

    ffn1_gu = _merge_gate_up(ffn1_gate, ffn1_up)
    ffn2_gu = _merge_gate_up(ffn2_gate, ffn2_up)
    ffn1_d = _cast_bf16(ffn1_down, DOWN_CAST_ROWS, "cast_down")
    ffn2_d = _cast_bf16(ffn2_down, DOWN_CAST_ROWS, "cast_down")
    w_in_b = _cast_bf16(w_in, W_IN_CAST_ROWS, "cast_w_in")
    w_out_b = _cast_bf16(w_out, W_OUT_CAST_ROWS, "cast_w_out")
    rg_w_a_b, rg_w_x_b, pool_w_b = (w.astype(_bf16) for w in (rg_w_a, rg_w_x, pool_w))

    for l in range(depth):
        xf = _ffn(xf, row(norm_ffn1[l]), ffn1_gu, ffn1_d, l, final_gain, False)
        qkv, rest = _mix_in(xf, row(norm_mix[l]), w_in_b, l)
        att = _attention(qkv, batch, seq)
        rp = _rnn_pool(rest, conv_w[l], row(conv_b[l]), rg_w_a_b[l], row(rg_b_a[l]),
                       rg_w_x_b[l], row(rg_b_x[l]), row(rg_lambda[l]),
                       pool_w_b[l], row(pool_scale[l]), batch, seq)
        xf = _mix_out(xf, att, rp, w_out_b, l)
        xf = _ffn(xf, row(norm_ffn2[l]), ffn2_gu, ffn2_d, l, final_gain, l == depth - 1)
    return xf.reshape(batch, seq, D_MODEL)
```

```python
import functools
import math

import jax
import jax.numpy as jnp
from jax import lax
from jax.experimental import pallas as pl
from jax.experimental.pallas import tpu as pltpu

D_MODEL = 2048
D_ATT = 1024
ATT_HEADS = 8
HEAD_DIM = 128
D_RNN = 512
RNN_HEADS = 4
D_POOL = 512
POOL_WINDOWS = (2, 4, 8, 16)
D_QKV = 3 * D_ATT
D_REST = 2 * D_RNN + D_POOL
D_IN_PROJ = D_QKV + D_REST
CONV_WIDTH = 4
RG_C = 8.0
D_FF = 5632
NORM_EPS = 1e-6

VMEM_LIMIT_BYTES = 60 * 1024 * 1024

FFN_TOKEN_TILE = 1024
FFN_HIDDEN_TILE = 512
MIX_TOKEN_TILE = 512
ATT_BLOCK = 256
ATT_HEADS_PER_STEP = 4
ATT_LOG2_CUTOFF = -100.0 * math.log2(math.e)
MASKED_LOGIT = 1e30
SEQ_TILE = 512
HISTORY_ROWS = 16
SUBLANES = 8
OUT_CHUNK = 512
DOWN_CAST_ROWS = 1408
W_IN_CAST_ROWS = 512
W_OUT_CAST_ROWS = 1024

_bf16 = jnp.bfloat16
_f32 = jnp.float32


def _rms_norm(x, gain):
    ms = jnp.mean(x * x, axis=-1, keepdims=True)
    return x * lax.rsqrt(ms + NORM_EPS) * gain


def _log_sigmoid(x):
    return jnp.minimum(x, 0.0) - jnp.log1p(jnp.exp(-jnp.abs(x)))


def _sigmoid(x):
    return 1.0 / (1.0 + jnp.exp(-x))


def _cast_kernel(w_ref, o_ref):
    o_ref[...] = w_ref[...].astype(_bf16)


def _cast_bf16(w, block_rows, name):
    depth, rows, cols = w.shape
    return pl.pallas_call(
        _cast_kernel,
        grid=(depth, rows // block_rows),
        in_specs=[pl.BlockSpec((None, block_rows, cols), lambda l, r: (l, r, 0))],
        out_specs=pl.BlockSpec((None, block_rows, cols), lambda l, r: (l, r, 0)),
        out_shape=jax.ShapeDtypeStruct(w.shape, _bf16),
        compiler_params=pltpu.CompilerParams(
            dimension_semantics=("parallel", "parallel"), vmem_limit_bytes=VMEM_LIMIT_BYTES),
        name=name,
    )(w)


def _merge_gate_up_kernel(g_ref, u_ref, o_ref):
    tf = g_ref.shape[-1]
    o_ref[:, :tf] = g_ref[...].astype(_bf16)
    o_ref[:, tf:] = u_ref[...].astype(_bf16)


def _merge_gate_up(gate, up):
    depth = gate.shape[0]
    tf = FFN_HIDDEN_TILE
    spec = pl.BlockSpec((None, D_MODEL, tf), lambda l, j: (l, 0, j))
    return pl.pallas_call(
        _merge_gate_up_kernel,
        grid=(depth, D_FF // tf),
        in_specs=[spec, spec],
        out_specs=pl.BlockSpec((None, D_MODEL, 2 * tf), lambda l, j: (l, 0, j)),
        out_shape=jax.ShapeDtypeStruct((depth, D_MODEL, 2 * D_FF), _bf16),
        compiler_params=pltpu.CompilerParams(
            dimension_semantics=("parallel", "parallel"), vmem_limit_bytes=VMEM_LIMIT_BYTES),
        name="merge_gate_up",
    )(gate, up)


def _ffn_kernel(x_ref, gain_ref, wgu_ref, wd_ref, fgain_ref, o_ref, h_ref, *, final_norm):
    j = pl.program_id(1)
    tf = FFN_HIDDEN_TILE

    def hidden_tile_contribution():
        gu = jnp.dot(h_ref[...], wgu_ref[...], preferred_element_type=_f32)
        g = gu[:, :tf]
        u = gu[:, tf:]
        a = (g * _sigmoid(g) * u).astype(_bf16)
        return jnp.dot(a, wd_ref[...], preferred_element_type=_f32)

    @pl.when(j == 0)
    def _():
        h_ref[...] = _rms_norm(x_ref[...], gain_ref[...]).astype(_bf16)
        o_ref[...] = hidden_tile_contribution()

    @pl.when(j > 0)
    def _():
        o_ref[...] += hidden_tile_contribution()

    @pl.when(j == pl.num_programs(1) - 1)
    def _():
        y = x_ref[...] + 0.5 * o_ref[...]
        if final_norm:
            y = _rms_norm(y, fgain_ref[...])
        o_ref[...] = y


def _ffn(x, gain, wgu, wd, layer, final_gain, final_norm):
    t = x.shape[0]
    tm, tf = FFN_TOKEN_TILE, FFN_HIDDEN_TILE
    return pl.pallas_call(
        functools.partial(_ffn_kernel, final_norm=final_norm),
        grid=(t // tm, D_FF // tf),
        in_specs=[
            pl.BlockSpec((tm, D_MODEL), lambda i, j: (i, 0)),
            pl.BlockSpec((1, D_MODEL), lambda i, j: (0, 0)),
            pl.BlockSpec((None, D_MODEL, 2 * tf), lambda i, j: (layer, 0, j)),
            pl.BlockSpec((None, tf, D_MODEL), lambda i, j: (layer, j, 0)),
            pl.BlockSpec((1, D_MODEL), lambda i, j: (0, 0)),
        ],
        out_specs=pl.BlockSpec((tm, D_MODEL), lambda i, j: (i, 0)),
        out_shape=jax.ShapeDtypeStruct((t, D_MODEL), _f32),
        scratch_shapes=[pltpu.VMEM((tm, D_MODEL), _bf16)],
        compiler_params=pltpu.CompilerParams(
            dimension_semantics=("parallel", "arbitrary"),
            vmem_limit_bytes=VMEM_LIMIT_BYTES),
        name="ffn",
    )(x, gain, wgu, wd, final_gain)


def _mix_in_kernel(x_ref, gain_ref, w_ref, qkv_ref, rest_ref):
    h = _rms_norm(x_ref[...], gain_ref[...]).astype(_bf16)
    qkv_ref[...] = jnp.dot(h, w_ref[:, :D_QKV], preferred_element_type=_f32).astype(_bf16)
    rest_ref[...] = jnp.dot(h, w_ref[:, D_QKV:], preferred_element_type=_f32)


def _mix_in(x, gain, w_in, layer):
    t = x.shape[0]
    tm = MIX_TOKEN_TILE
    return pl.pallas_call(
        _mix_in_kernel,
        grid=(t // tm,),
        in_specs=[
            pl.BlockSpec((tm, D_MODEL), lambda i: (i, 0)),
            pl.BlockSpec((1, D_MODEL), lambda i: (0, 0)),
            pl.BlockSpec((None, D_MODEL, D_IN_PROJ), lambda i: (layer, 0, 0),
                         pipeline_mode=pl.Buffered(1)),
        ],
        out_specs=[
            pl.BlockSpec((tm, D_QKV), lambda i: (i, 0)),
            pl.BlockSpec((tm, D_REST), lambda i: (i, 0)),
        ],
        out_shape=[jax.ShapeDtypeStruct((t, D_QKV), _bf16),
                   jax.ShapeDtypeStruct((t, D_REST), _f32)],
        compiler_params=pltpu.CompilerParams(
            dimension_semantics=("parallel",), vmem_limit_bytes=VMEM_LIMIT_BYTES),
        name="mix_in",
    )(x, gain, w_in)


def _log2_fail(y):
    return -(jnp.maximum(y, 0.0) + jnp.log2(1.0 + jnp.exp2(-jnp.abs(y))))


def _split_bf16(x):
    hi = x.astype(_bf16)
    lo = (x - hi.astype(_f32)).astype(_bf16)
    return hi, lo


def _attention_kernel(q_ref, k_ref, v_ref, o_ref, acc_ref, carry_ref):
    i = pl.program_id(2)
    blk = ATT_BLOCK
    logit_scale = HEAD_DIM ** -0.5 * math.log2(math.e)
    row = lax.broadcasted_iota(jnp.int32, (blk, blk), 0)
    col = lax.broadcasted_iota(jnp.int32, (blk, blk), 1)
    upper = (row > col).astype(_bf16)
    contract_last = (((1,), (1,)), ((), ()))

    first = jnp.maximum(i - 1, 0)
    start = pl.multiple_of(first * blk, blk)
    row2 = lax.broadcasted_iota(jnp.int32, (blk, 2 * blk), 0)
    col2 = lax.broadcasted_iota(jnp.int32, (blk, 2 * blk), 1)
    causal = col2 - row2 < (i - first) * blk

    heads = range(ATT_HEADS_PER_STEP)
    head_lanes = [slice(hd * HEAD_DIM, (hd + 1) * HEAD_DIM) for hd in heads]
    ys, lfs, css = [], [], []
    for lanes in head_lanes:
        k2 = k_ref[pl.ds(start, 2 * blk), lanes]
        y = lax.dot_general(q_ref[:, lanes], k2, contract_last,
                            preferred_element_type=_f32) * logit_scale
        ys.append(jnp.where(causal, y, -MASKED_LOGIT))
    for y in ys:
        lf = _log2_fail(y)
        hi, lo = _split_bf16(lf)
        stacked = jnp.concatenate([hi[:, blk:], lo[:, blk:], hi[:, :blk], lo[:, :blk]], axis=0)
        lfs.append(lf)
        css.append(jnp.dot(stacked, upper, preferred_element_type=_f32))
    for hd, lanes in zip(heads, head_lanes):
        y, lf, cs = ys[hd], lfs[hd], css[hd]
        after_r = cs[0:blk] + cs[blk:2 * blk]
        total_r = after_r[:, 0:1] + lf[:, blk:blk + 1]
        after_l = cs[2 * blk:3 * blk] + cs[3 * blk:] + total_r
        after = jnp.concatenate([after_l, after_r], axis=1)
        w = jnp.exp2(y + lf + after)
        v2 = v_ref[pl.ds(start, 2 * blk), lanes]
        acc_ref[hd] = jnp.dot(w.astype(_bf16), v2, preferred_element_type=_f32)
        carry_ref[hd] = after_l[:, 0:1] + lf[:, 0:1]

    def walk_earlier_blocks(hd, lanes):
        def cond(jb):
            return jnp.logical_and(jb >= 0, jnp.max(carry_ref[hd]) > ATT_LOG2_CUTOFF)

        def body(jb):
            kstart = pl.multiple_of(jb * blk, blk)
            kb = k_ref[pl.ds(kstart, blk), lanes]
            vb = v_ref[pl.ds(kstart, blk), lanes]
            yb = lax.dot_general(q_ref[:, lanes], kb, contract_last,
                                 preferred_element_type=_f32) * logit_scale
            lfb = _log2_fail(yb)
            hb, lb = _split_bf16(lfb)
            csb = jnp.dot(jnp.concatenate([hb, lb], axis=0), upper,
                          preferred_element_type=_f32)
            after_b = csb[0:blk] + csb[blk:] + carry_ref[hd]
            wb = jnp.exp2(yb + lfb + after_b)
            acc_ref[hd] += jnp.dot(wb.astype(_bf16), vb, preferred_element_type=_f32)
            carry_ref[hd] = after_b[:, 0:1] + lfb[:, 0:1]
            return jb - 1

        lax.while_loop(cond, body, first - 1)

    @pl.when(jnp.logical_and(first >= 1, jnp.max(carry_ref[...]) > ATT_LOG2_CUTOFF))
    def _():
        for hd, lanes in zip(heads, head_lanes):
            walk_earlier_blocks(hd, lanes)

    for hd, lanes in zip(heads, head_lanes):
        o_ref[:, lanes] = acc_ref[hd].astype(o_ref.dtype)


def _attention(qkv, batch, seq):
    blk = ATT_BLOCK
    hps = ATT_HEADS_PER_STEP
    groups = ATT_HEADS // hps
    width = hps * HEAD_DIM
    qkv3 = qkv.reshape(batch, seq, D_QKV)
    out = pl.pallas_call(
        _attention_kernel,
        grid=(batch, groups, seq // blk),
        in_specs=[
            pl.BlockSpec((None, blk, width), lambda b, g, i: (b, i, g)),
            pl.BlockSpec((None, seq, width), lambda b, g, i: (b, 0, groups + g)),
            pl.BlockSpec((None, seq, width), lambda b, g, i: (b, 0, 2 * groups + g)),
        ],
        out_specs=pl.BlockSpec((None, blk, width), lambda b, g, i: (b, i, g)),
        out_shape=jax.ShapeDtypeStruct((batch, seq, D_ATT), _bf16),
        scratch_shapes=[pltpu.VMEM((hps, blk, HEAD_DIM), _f32),
                        pltpu.VMEM((hps, blk, 1), _f32)],
        compiler_params=pltpu.CompilerParams(
            dimension_semantics=("parallel", "parallel", "arbitrary"),
            vmem_limit_bytes=VMEM_LIMIT_BYTES),
        name="attention",
    )(qkv3, qkv3, qkv3)
    return out.reshape(batch * seq, D_ATT)


def _scan_rows(a, b, n):
    t_idx = lax.broadcasted_iota(jnp.int32, a.shape, 0)
    d = 1
    while d < n:
        keep = t_idx >= d
        a_sh = jnp.where(keep, pltpu.roll(a, d, 0), 1.0)
        b_sh = jnp.where(keep, pltpu.roll(b, d, 0), 0.0)
        b = a * b_sh + b
        a = a * a_sh
        d *= 2
    return a, b


def _mix_out_kernel(x_ref, att_ref, rest_ref, w_ref, conv_w_ref, conv_b_ref, wa_ref, ba_ref,
                    wx_ref, bx_ref, lam_ref, pool_w_ref, pool_scale_ref, o_ref,
                    xr_ext, xp_ext, h_ref, a_ref, b_ref, c_ref, hs_ref, rp_ref):
    s = pl.program_id(1)
    tc = SEQ_TILE
    hist = HISTORY_ROWS
    grp = SUBLANES
    n_grp = tc // grp

    @pl.when(s == 0)
    def _():
        xr_ext[0:hist, :] = jnp.zeros((hist, D_RNN), _f32)
        xp_ext[0:hist, :] = jnp.zeros((hist, D_POOL), _f32)
        h_ref[...] = jnp.zeros_like(h_ref)
        rp_ref[...] = jnp.zeros_like(rp_ref)

    def project_previous_tile(chunk):
        cols = slice(chunk * OUT_CHUNK, (chunk + 1) * OUT_CHUNK)
        y = jnp.dot(att_ref[...], w_ref[:D_ATT, cols], preferred_element_type=_f32)
        y = y + jnp.dot(rp_ref[...], w_ref[D_ATT:, cols], preferred_element_type=_f32)
        o_ref[:, cols] = x_ref[:, cols] + y

    project_previous_tile(0)
    xg = rest_ref[:, 0:D_RNN]
    xr_ext[hist:, :] = rest_ref[:, D_RNN:2 * D_RNN]
    xp_ext[hist:, :] = rest_ref[:, 2 * D_RNN:]

    u = jnp.zeros((tc, D_RNN), _f32) + conv_b_ref[...]
    for j in range(CONV_WIDTH):
        off = hist - (CONV_WIDTH - 1) + j
        u = u + conv_w_ref[j:j + 1, :] * xr_ext[off:off + tc, :]

    project_previous_tile(1)
    ub = u.astype(_bf16)
    r_parts, i_parts = [], []
    for hd in range(RNN_HEADS):
        uh = ub[:, hd * HEAD_DIM:(hd + 1) * HEAD_DIM]
        r_parts.append(jnp.dot(uh, wa_ref[hd], preferred_element_type=_f32))
        i_parts.append(jnp.dot(uh, wx_ref[hd], preferred_element_type=_f32))
    r = _sigmoid(jnp.concatenate(r_parts, axis=1) + ba_ref[...])
    ig = _sigmoid(jnp.concatenate(i_parts, axis=1) + bx_ref[...])
    log_a = RG_C * r * _log_sigmoid(lam_ref[...])
    a = jnp.exp(log_a)
    var = -jnp.tanh(log_a) * (1.0 + a * a)
    b = jnp.where(var > 0.0, var * lax.rsqrt(var), 0.0) * (ig * u)

    project_previous_tile(2)
    a3 = a.reshape(n_grp, grp, D_RNN)
    b3 = b.reshape(n_grp, grp, D_RNN)
    sub = lax.broadcasted_iota(jnp.int32, (n_grp, grp, D_RNN), 1)
    d = 1
    while d < grp:
        keep = sub >= d
        a_sh = jnp.where(keep, pltpu.roll(a3, d, 1), 1.0)
        b_sh = jnp.where(keep, pltpu.roll(b3, d, 1), 0.0)
        b3 = a3 * b_sh + b3
        a3 = a3 * a_sh
        d *= 2
    a = a3.reshape(tc, D_RNN)
    b = b3.reshape(tc, D_RNN)
    project_previous_tile(3)
    g_idx = lax.broadcasted_iota(jnp.int32, (n_grp, HEAD_DIM), 0)
    group_ends = pl.ds(grp - 1, n_grp, stride=grp)
    for hd in range(RNN_HEADS):
        lanes = slice(hd * HEAD_DIM, (hd + 1) * HEAD_DIM)
        a_ref[hd] = a[:, lanes]
        b_ref[hd] = b[:, lanes]
        a_end, b_end = _scan_rows(a_ref[hd, group_ends, :], b_ref[hd, group_ends, :], n_grp)
        h_in = h_ref[:, lanes]
        h_end = a_end * h_in + b_end
        c_ref[hd] = jnp.where(g_idx == 0, h_in, pltpu.roll(h_end, 1, 0))
        h_ref[:, lanes] = h_end[n_grp - 1:n_grp, :]
        for k in range(n_grp):
            rows = slice(k * grp, (k + 1) * grp)
            hs_ref[rows, lanes] = (a_ref[hd, rows, :] * c_ref[hd, k:k + 1, :]
                                   + b_ref[hd, rows, :])

    c0 = math.sqrt(2.0 / math.pi)
    gelu = 0.5 * xg * (1.0 + jnp.tanh(c0 * (xg + 0.044715 * (xg * xg * xg))))
    rp_ref[:, 0:D_RNN] = (gelu * hs_ref[...]).astype(_bf16)

    tile = jnp.minimum(s, pl.num_programs(1) - 2)
    pos = tile * tc + lax.broadcasted_iota(jnp.int32, (tc, HEAD_DIM), 0)
    for g, win in enumerate(POOL_WINDOWS):
        lo, hi = g * HEAD_DIM, (g + 1) * HEAD_DIM
        tot = xp_ext[:, lo:hi]
        span = 1
        while span < win:
            tot = tot + pltpu.roll(tot, span, 0)
            span *= 2
        cur = xp_ext[hist:, lo:hi]
        count = jnp.minimum(pos + 1, win).astype(_f32)
        dlt = (tot[hist:, :] / count - cur).astype(_bf16)
        y = jnp.dot(dlt, pool_w_ref[g], preferred_element_type=_f32)
        rp_ref[:, D_RNN + lo:D_RNN + hi] = (y * pool_scale_ref[:, lo:hi]).astype(_bf16)

    xr_ext[0:hist, :] = xr_ext[tc:tc + hist, :]
    xp_ext[0:hist, :] = xp_ext[tc:tc + hist, :]


def _mix_out(x, att, rest, w_out, layer, conv_w, conv_b, w_a, b_a, w_x, b_x, lam, pool_w,
             pool_scale, batch, seq):
    tc = SEQ_TILE
    n_grp = tc // SUBLANES
    n_tiles = seq // tc
    lagged = lambda width: pl.BlockSpec(
        (None, tc, width), lambda b, s: (b, jnp.maximum(s - 1, 0), 0))
    current = lambda width: pl.BlockSpec(
        (None, tc, width), lambda b, s: (b, jnp.minimum(s, n_tiles - 1), 0))
    vec = lambda n: pl.BlockSpec((1, n), lambda b, s: (0, 0))
    mats = pl.BlockSpec((RNN_HEADS, HEAD_DIM, HEAD_DIM), lambda b, s: (0, 0, 0))
    out = pl.pallas_call(
        _mix_out_kernel,
        grid=(batch, n_tiles + 1),
        in_specs=[
            lagged(D_MODEL), lagged(D_ATT), current(D_REST),
            pl.BlockSpec((None, D_MODEL, D_MODEL), lambda b, s: (layer, 0, 0),
                         pipeline_mode=pl.Buffered(1)),
            pl.BlockSpec((CONV_WIDTH, D_RNN), lambda b, s: (0, 0)),
            vec(D_RNN), mats, vec(D_RNN), mats, vec(D_RNN), vec(D_RNN),
            mats, vec(D_POOL),
        ],
        out_specs=lagged(D_MODEL),
        out_shape=jax.ShapeDtypeStruct((batch, seq, D_MODEL), _f32),
        scratch_shapes=[
            pltpu.VMEM((HISTORY_ROWS + tc, D_RNN), _f32),
            pltpu.VMEM((HISTORY_ROWS + tc, D_POOL), _f32),
            pltpu.VMEM((1, D_RNN), _f32),
            pltpu.VMEM((RNN_HEADS, tc, HEAD_DIM), _f32),
            pltpu.VMEM((RNN_HEADS, tc, HEAD_DIM), _f32),
            pltpu.VMEM((RNN_HEADS, n_grp, HEAD_DIM), _f32),
            pltpu.VMEM((tc, D_RNN), _f32),
            pltpu.VMEM((tc, D_RNN + D_POOL), _bf16),
        ],
        compiler_params=pltpu.CompilerParams(
            dimension_semantics=("parallel", "arbitrary"),
            vmem_limit_bytes=VMEM_LIMIT_BYTES),
        name="mix_out",
    )(x.reshape(batch, seq, D_MODEL), att.reshape(batch, seq, D_ATT),
      rest.reshape(batch, seq, D_REST), w_out, conv_w, conv_b, w_a, b_a, w_x, b_x, lam,
      pool_w, pool_scale)
    return out.reshape(batch * seq, D_MODEL)


def kernel(x, norm_ffn1, ffn1_gate, ffn1_up, ffn1_down, norm_mix, w_in, conv_w, conv_b, rg_w_a, rg_b_a, rg_w_x, rg_b_x, rg_lambda, pool_w, pool_scale, w_out, norm_ffn2, ffn2_gate, ffn2_up, ffn2_down, norm_final):
    batch, seq, _ = x.shape
    depth = w_in.shape[0]
    assert x.shape == (batch, seq, D_MODEL) and seq % max(2 * ATT_BLOCK, SEQ_TILE) == 0
    assert (batch * seq) % max(FFN_TOKEN_TILE, MIX_TOKEN_TILE) == 0
    xf = x.reshape(batch * seq, D_MODEL)
    row = lambda v: v.reshape(1, -1)
    final_gain = row(norm_final)

    ffn1_gu = _merge_gate_up(ffn1_gate, ffn1_up)
    ffn2_gu = _merge_gate_up(ffn2_gate, ffn2_up)
    ffn1_d = _cast_bf16(ffn1_down, DOWN_CAST_ROWS, "cast_down")
    ffn2_d = _cast_bf16(ffn2_down, DOWN_CAST_ROWS, "cast_down")
    w_in_b = _cast_bf16(w_in, W_IN_CAST_ROWS, "cast_w_in")
    w_out_b = _cast_bf16(w_out, W_OUT_CAST_ROWS, "cast_w_out")
    rg_w_a_b, rg_w_x_b, pool_w_b = (w.astype(_bf16) for w in (rg_w_a, rg_w_x, pool_w))

    for l in range(depth):
        xf = _ffn(xf, row(norm_ffn1[l]), ffn1_gu, ffn1_d, l, final_gain, False)
        qkv, rest = _mix_in(xf, row(norm_mix[l]), w_in_b, l)
        att = _attention(qkv, batch, seq)
        xf = _mix_out(xf, att, rest, w_out_b, l, conv_w[l], row(conv_b[l]), rg_w_a_b[l],
                      row(rg_b_a[l]), rg_w_x_b[l], row(rg_b_x[l]), row(rg_lambda[l]),
                      pool_w_b[l], row(pool_scale[l]), batch, seq)
        xf = _ffn(xf, row(norm_ffn2[l]), ffn2_gu, ffn2_d, l, final_gain, l == depth - 1)
    return xf.reshape(batch, seq, D_MODEL)
```

```python
import functools
import math

import jax
import jax.numpy as jnp
from jax import lax
from jax.experimental import pallas as pl
from jax.experimental.pallas import tpu as pltpu

D_MODEL = 2048
D_ATT = 1024
ATT_HEADS = 8
HEAD_DIM = 128
D_RNN = 512
RNN_HEADS = 4
D_POOL = 512
POOL_WINDOWS = (2, 4, 8, 16)
D_QKV = 3 * D_ATT
D_REST = 2 * D_RNN + D_POOL
D_IN_PROJ = D_QKV + D_REST
CONV_WIDTH = 4
RG_C = 8.0
D_FF = 5632
NORM_EPS = 1e-6

VMEM_LIMIT_BYTES = 60 * 1024 * 1024

FFN_TOKEN_TILE = 1024
FFN_HIDDEN_TILE = 512
MIX_TOKEN_TILE = 512
ATT_BLOCK = 256
ATT_HEADS_PER_STEP = 4
ATT_LOG2_CUTOFF = -100.0 * math.log2(math.e)
MASKED_LOGIT = 1e30
SEQ_TILE = 512
HISTORY_ROWS = 16
SUBLANES = 8
OUT_CHUNK = 512
DOWN_CAST_ROWS = 1408
W_IN_CAST_ROWS = 512
W_OUT_CAST_ROWS = 1024

_bf16 = jnp.bfloat16
_f32 = jnp.float32


def _rms_norm(x, gain):
    ms = jnp.mean(x * x, axis=-1, keepdims=True)
    return x * lax.rsqrt(ms + NORM_EPS) * gain


def _log_sigmoid(x):
    return jnp.minimum(x, 0.0) - jnp.log1p(jnp.exp(-jnp.abs(x)))


def _sigmoid(x):
    return 1.0 / (1.0 + jnp.exp(-x))


def _cast_kernel(w_ref, o_ref):
    o_ref[...] = w_ref[...].astype(_bf16)


def _cast_layer_bf16(w, layer, block_rows, name):
    _, rows, cols = w.shape
    return pl.pallas_call(
        _cast_kernel,
        grid=(rows // block_rows,),
        in_specs=[pl.BlockSpec((None, block_rows, cols), lambda r: (layer, r, 0))],
        out_specs=pl.BlockSpec((block_rows, cols), lambda r: (r, 0)),
        out_shape=jax.ShapeDtypeStruct((rows, cols), _bf16),
        compiler_params=pltpu.CompilerParams(
            dimension_semantics=("parallel",), vmem_limit_bytes=VMEM_LIMIT_BYTES),
        name=name,
    )(w)


def _cast_bf16(w, block_rows, name):
    depth, rows, cols = w.shape
    return pl.pallas_call(
        _cast_kernel,
        grid=(depth, rows // block_rows),
        in_specs=[pl.BlockSpec((None, block_rows, cols), lambda l, r: (l, r, 0))],
        out_specs=pl.BlockSpec((None, block_rows, cols), lambda l, r: (l, r, 0)),
        out_shape=jax.ShapeDtypeStruct(w.shape, _bf16),
        compiler_params=pltpu.CompilerParams(
            dimension_semantics=("parallel", "parallel"), vmem_limit_bytes=VMEM_LIMIT_BYTES),
        name=name,
    )(w)


def _merge_gate_up_kernel(g_ref, u_ref, o_ref):
    tf = g_ref.shape[-1]
    o_ref[:, :tf] = g_ref[...].astype(_bf16)
    o_ref[:, tf:] = u_ref[...].astype(_bf16)


def _merge_gate_up(gate, up, layer):
    tf = FFN_HIDDEN_TILE
    spec = pl.BlockSpec((None, D_MODEL, tf), lambda j: (layer, 0, j))
    return pl.pallas_call(
        _merge_gate_up_kernel,
        grid=(D_FF // tf,),
        in_specs=[spec, spec],
        out_specs=pl.BlockSpec((D_MODEL, 2 * tf), lambda j: (0, j)),
        out_shape=jax.ShapeDtypeStruct((D_MODEL, 2 * D_FF), _bf16),
        compiler_params=pltpu.CompilerParams(
            dimension_semantics=("parallel",), vmem_limit_bytes=VMEM_LIMIT_BYTES),
        name="merge_gate_up",
    )(gate, up)


def _ffn_kernel(x_ref, gain_ref, wgu_ref, wd_ref, fgain_ref, *rest, final_norm, cast_next):
    if cast_next:
        ng_ref, nu_ref, nd_ref, o_ref, ngu_ref, ndn_ref, h_ref = rest
        _merge_gate_up_kernel(ng_ref, nu_ref, ngu_ref)
        _cast_kernel(nd_ref, ndn_ref)
    else:
        o_ref, h_ref = rest
    j = pl.program_id(1)
    tf = FFN_HIDDEN_TILE

    def hidden_tile_contribution():
        gu = jnp.dot(h_ref[...], wgu_ref[...], preferred_element_type=_f32)
        g = gu[:, :tf]
        u = gu[:, tf:]
        a = (g * _sigmoid(g) * u).astype(_bf16)
        return jnp.dot(a, wd_ref[...], preferred_element_type=_f32)

    @pl.when(j == 0)
    def _():
        h_ref[...] = _rms_norm(x_ref[...], gain_ref[...]).astype(_bf16)
        o_ref[...] = hidden_tile_contribution()

    @pl.when(j > 0)
    def _():
        o_ref[...] += hidden_tile_contribution()

    @pl.when(j == pl.num_programs(1) - 1)
    def _():
        y = x_ref[...] + 0.5 * o_ref[...]
        if final_norm:
            y = _rms_norm(y, fgain_ref[...])
        o_ref[...] = y


def _ffn(x, gain, wgu, wd, final_gain, final_norm, next_weights=None):
    t = x.shape[0]
    tm, tf = FFN_TOKEN_TILE, FFN_HIDDEN_TILE
    n_tok = t // tm
    in_specs = [
        pl.BlockSpec((tm, D_MODEL), lambda i, j: (i, 0)),
        pl.BlockSpec((1, D_MODEL), lambda i, j: (0, 0)),
        pl.BlockSpec((D_MODEL, 2 * tf), lambda i, j: (0, j)),
        pl.BlockSpec((tf, D_MODEL), lambda i, j: (j, 0)),
        pl.BlockSpec((1, D_MODEL), lambda i, j: (0, 0)),
    ]
    out_specs = [pl.BlockSpec((tm, D_MODEL), lambda i, j: (i, 0))]
    out_shape = [jax.ShapeDtypeStruct((t, D_MODEL), _f32)]
    operands = [x, gain, wgu, wd, final_gain]
    if next_weights is not None:
        gate, up, down, nl = next_weights
        slab = D_MODEL // n_tok
        assert slab * n_tok == D_MODEL and slab % 128 == 0
        in_specs += [
            pl.BlockSpec((None, slab, tf), lambda i, j: (nl, i, j)),
            pl.BlockSpec((None, slab, tf), lambda i, j: (nl, i, j)),
            pl.BlockSpec((None, tf, slab), lambda i, j: (nl, j, i)),
        ]
        out_specs += [pl.BlockSpec((slab, 2 * tf), lambda i, j: (i, j)),
                      pl.BlockSpec((tf, slab), lambda i, j: (j, i))]
        out_shape += [jax.ShapeDtypeStruct((D_MODEL, 2 * D_FF), _bf16),
                      jax.ShapeDtypeStruct((D_FF, D_MODEL), _bf16)]
        operands += [gate, up, down]
    outs = pl.pallas_call(
        functools.partial(_ffn_kernel, final_norm=final_norm,
                          cast_next=next_weights is not None),
        grid=(n_tok, D_FF // tf),
        in_specs=in_specs,
        out_specs=out_specs,
        out_shape=out_shape,
        scratch_shapes=[pltpu.VMEM((tm, D_MODEL), _bf16)],
        compiler_params=pltpu.CompilerParams(
            dimension_semantics=("parallel", "arbitrary"),
            vmem_limit_bytes=VMEM_LIMIT_BYTES),
        name="ffn",
    )(*operands)
    return outs if next_weights is not None else outs[0]


def _mix_in_kernel(x_ref, gain_ref, w_ref, qkv_ref, rest_ref):
    h = _rms_norm(x_ref[...], gain_ref[...]).astype(_bf16)
    qkv_ref[...] = jnp.dot(h, w_ref[:, :D_QKV], preferred_element_type=_f32).astype(_bf16)
    rest_ref[...] = jnp.dot(h, w_ref[:, D_QKV:], preferred_element_type=_f32)


def _mix_in(x, gain, w_in, layer):
    t = x.shape[0]
    tm = MIX_TOKEN_TILE
    return pl.pallas_call(
        _mix_in_kernel,
        grid=(t // tm,),
        in_specs=[
            pl.BlockSpec((tm, D_MODEL), lambda i: (i, 0)),
            pl.BlockSpec((1, D_MODEL), lambda i: (0, 0)),
            pl.BlockSpec((None, D_MODEL, D_IN_PROJ), lambda i: (layer, 0, 0),
                         pipeline_mode=pl.Buffered(1)),
        ],
        out_specs=[
            pl.BlockSpec((tm, D_QKV), lambda i: (i, 0)),
            pl.BlockSpec((tm, D_REST), lambda i: (i, 0)),
        ],
        out_shape=[jax.ShapeDtypeStruct((t, D_QKV), _bf16),
                   jax.ShapeDtypeStruct((t, D_REST), _f32)],
        compiler_params=pltpu.CompilerParams(
            dimension_semantics=("parallel",), vmem_limit_bytes=VMEM_LIMIT_BYTES),
        name="mix_in",
    )(x, gain, w_in)


def _log2_fail(y):
    return -(jnp.maximum(y, 0.0) + jnp.log2(1.0 + jnp.exp2(-jnp.abs(y))))


def _split_bf16(x):
    hi = x.astype(_bf16)
    lo = (x - hi.astype(_f32)).astype(_bf16)
    return hi, lo


def _attention_kernel(q_ref, k_ref, v_ref, o_ref, acc_ref, carry_ref):
    i = pl.program_id(2)
    blk = ATT_BLOCK
    logit_scale = HEAD_DIM ** -0.5 * math.log2(math.e)
    row = lax.broadcasted_iota(jnp.int32, (blk, blk), 0)
    col = lax.broadcasted_iota(jnp.int32, (blk, blk), 1)
    upper = (row > col).astype(_bf16)
    contract_last = (((1,), (1,)), ((), ()))

    first = jnp.maximum(i - 1, 0)
    start = pl.multiple_of(first * blk, blk)
    row2 = lax.broadcasted_iota(jnp.int32, (blk, 2 * blk), 0)
    col2 = lax.broadcasted_iota(jnp.int32, (blk, 2 * blk), 1)
    causal = col2 - row2 < (i - first) * blk

    heads = range(ATT_HEADS_PER_STEP)
    head_lanes = [slice(hd * HEAD_DIM, (hd + 1) * HEAD_DIM) for hd in heads]
    ys, lfs, css = [], [], []
    for lanes in head_lanes:
        k2 = k_ref[pl.ds(start, 2 * blk), lanes]
        y = lax.dot_general(q_ref[:, lanes], k2, contract_last,
                            preferred_element_type=_f32) * logit_scale
        ys.append(jnp.where(causal, y, -MASKED_LOGIT))
    for y in ys:
        lf = _log2_fail(y)
        hi, lo = _split_bf16(lf)
        stacked = jnp.concatenate([hi[:, blk:], lo[:, blk:], hi[:, :blk], lo[:, :blk]], axis=0)
        lfs.append(lf)
        css.append(jnp.dot(stacked, upper, preferred_element_type=_f32))
    for hd, lanes in zip(heads, head_lanes):
        y, lf, cs = ys[hd], lfs[hd], css[hd]
        after_r = cs[0:blk] + cs[blk:2 * blk]
        total_r = after_r[:, 0:1] + lf[:, blk:blk + 1]
        after_l = cs[2 * blk:3 * blk] + cs[3 * blk:] + total_r
        after = jnp.concatenate([after_l, after_r], axis=1)
        w = jnp.exp2(y + lf + after)
        v2 = v_ref[pl.ds(start, 2 * blk), lanes]
        acc_ref[hd] = jnp.dot(w.astype(_bf16), v2, preferred_element_type=_f32)
        carry_ref[hd] = after_l[:, 0:1] + lf[:, 0:1]

    def walk_earlier_blocks(hd, lanes):
        def cond(jb):
            return jnp.logical_and(jb >= 0, jnp.max(carry_ref[hd]) > ATT_LOG2_CUTOFF)

        def body(jb):
            kstart = pl.multiple_of(jb * blk, blk)
            kb = k_ref[pl.ds(kstart, blk), lanes]
            vb = v_ref[pl.ds(kstart, blk), lanes]
            yb = lax.dot_general(q_ref[:, lanes], kb, contract_last,
                                 preferred_element_type=_f32) * logit_scale
            lfb = _log2_fail(yb)
            hb, lb = _split_bf16(lfb)
            csb = jnp.dot(jnp.concatenate([hb, lb], axis=0), upper,
                          preferred_element_type=_f32)
            after_b = csb[0:blk] + csb[blk:] + carry_ref[hd]
            wb = jnp.exp2(yb + lfb + after_b)
            acc_ref[hd] += jnp.dot(wb.astype(_bf16), vb, preferred_element_type=_f32)
            carry_ref[hd] = after_b[:, 0:1] + lfb[:, 0:1]
            return jb - 1

        lax.while_loop(cond, body, first - 1)

    @pl.when(jnp.logical_and(first >= 1, jnp.max(carry_ref[...]) > ATT_LOG2_CUTOFF))
    def _():
        for hd, lanes in zip(heads, head_lanes):
            walk_earlier_blocks(hd, lanes)

    for hd, lanes in zip(heads, head_lanes):
        o_ref[:, lanes] = acc_ref[hd].astype(o_ref.dtype)


def _attention(qkv, batch, seq):
    blk = ATT_BLOCK
    hps = ATT_HEADS_PER_STEP
    groups = ATT_HEADS // hps
    width = hps * HEAD_DIM
    qkv3 = qkv.reshape(batch, seq, D_QKV)
    out = pl.pallas_call(
        _attention_kernel,
        grid=(batch, groups, seq // blk),
        in_specs=[
            pl.BlockSpec((None, blk, width), lambda b, g, i: (b, i, g)),
            pl.BlockSpec((None, seq, width), lambda b, g, i: (b, 0, groups + g)),
            pl.BlockSpec((None, seq, width), lambda b, g, i: (b, 0, 2 * groups + g)),
        ],
        out_specs=pl.BlockSpec((None, blk, width), lambda b, g, i: (b, i, g)),
        out_shape=jax.ShapeDtypeStruct((batch, seq, D_ATT), _bf16),
        scratch_shapes=[pltpu.VMEM((hps, blk, HEAD_DIM), _f32),
                        pltpu.VMEM((hps, blk, 1), _f32)],
        compiler_params=pltpu.CompilerParams(
            dimension_semantics=("parallel", "parallel", "arbitrary"),
            vmem_limit_bytes=VMEM_LIMIT_BYTES),
        name="attention",
    )(qkv3, qkv3, qkv3)
    return out.reshape(batch * seq, D_ATT)


def _scan_rows(a, b, n):
    t_idx = lax.broadcasted_iota(jnp.int32, a.shape, 0)
    d = 1
    while d < n:
        keep = t_idx >= d
        a_sh = jnp.where(keep, pltpu.roll(a, d, 0), 1.0)
        b_sh = jnp.where(keep, pltpu.roll(b, d, 0), 0.0)
        b = a * b_sh + b
        a = a * a_sh
        d *= 2
    return a, b


def _mix_out_kernel(x_ref, att_ref, rest_ref, w_ref, conv_w_ref, conv_b_ref, wa_ref, ba_ref,
                    wx_ref, bx_ref, lam_ref, pool_w_ref, pool_scale_ref, o_ref,
                    xr_ext, xp_ext, h_ref, a_ref, b_ref, c_ref, hs_ref, rp_ref):
    s = pl.program_id(1)
    tc = SEQ_TILE
    hist = HISTORY_ROWS
    grp = SUBLANES
    n_grp = tc // grp

    @pl.when(s == 0)
    def _():
        xr_ext[0:hist, :] = jnp.zeros((hist, D_RNN), _f32)
        xp_ext[0:hist, :] = jnp.zeros((hist, D_POOL), _f32)
        h_ref[...] = jnp.zeros_like(h_ref)
        rp_ref[...] = jnp.zeros_like(rp_ref)

    def project_previous_tile(chunk):
        cols = slice(chunk * OUT_CHUNK, (chunk + 1) * OUT_CHUNK)
        y = jnp.dot(att_ref[...], w_ref[:D_ATT, cols], preferred_element_type=_f32)
        y = y + jnp.dot(rp_ref[...], w_ref[D_ATT:, cols], preferred_element_type=_f32)
        o_ref[:, cols] = x_ref[:, cols] + y

    project_previous_tile(0)
    xr_ext[hist:, :] = rest_ref[:, D_RNN:2 * D_RNN]
    xp_ext[hist:, :] = rest_ref[:, 2 * D_RNN:]

    u = jnp.zeros((tc, D_RNN), _f32) + conv_b_ref[...]
    for j in range(CONV_WIDTH):
        off = hist - (CONV_WIDTH - 1) + j
        u = u + conv_w_ref[j:j + 1, :] * xr_ext[off:off + tc, :]

    project_previous_tile(1)
    ub = u.astype(_bf16)
    r_parts, i_parts = [], []
    for hd in range(RNN_HEADS):
        uh = ub[:, hd * HEAD_DIM:(hd + 1) * HEAD_DIM]
        r_parts.append(jnp.dot(uh, wa_ref[hd], preferred_element_type=_f32))
        i_parts.append(jnp.dot(uh, wx_ref[hd], preferred_element_type=_f32))
    r = _sigmoid(jnp.concatenate(r_parts, axis=1) + ba_ref[...])
    ig = _sigmoid(jnp.concatenate(i_parts, axis=1) + bx_ref[...])
    log_a = RG_C * r * _log_sigmoid(lam_ref[...])
    a = jnp.exp(log_a)
    var = -jnp.tanh(log_a) * (1.0 + a * a)
    b = jnp.where(var > 0.0, var * lax.rsqrt(var), 0.0) * (ig * u)

    project_previous_tile(2)
    a3 = a.reshape(n_grp, grp, D_RNN)
    b3 = b.reshape(n_grp, grp, D_RNN)
    sub = lax.broadcasted_iota(jnp.int32, (n_grp, grp, D_RNN), 1)
    d = 1
    while d < grp:
        keep = sub >= d
        a_sh = jnp.where(keep, pltpu.roll(a3, d, 1), 1.0)
        b_sh = jnp.where(keep, pltpu.roll(b3, d, 1), 0.0)
        b3 = a3 * b_sh + b3
        a3 = a3 * a_sh
        d *= 2
    a = a3.reshape(tc, D_RNN)
    b = b3.reshape(tc, D_RNN)
    project_previous_tile(3)
    g_idx = lax.broadcasted_iota(jnp.int32, (n_grp, HEAD_DIM), 0)
    group_ends = pl.ds(grp - 1, n_grp, stride=grp)
    for hd in range(RNN_HEADS):
        lanes = slice(hd * HEAD_DIM, (hd + 1) * HEAD_DIM)
        a_ref[hd] = a[:, lanes]
        b_ref[hd] = b[:, lanes]
        a_end, b_end = _scan_rows(a_ref[hd, group_ends, :], b_ref[hd, group_ends, :], n_grp)
        h_in = h_ref[:, lanes]
        h_end = a_end * h_in + b_end
        c_ref[hd] = jnp.where(g_idx == 0, h_in, pltpu.roll(h_end, 1, 0))
        h_ref[:, lanes] = h_end[n_grp - 1:n_grp, :]
        for k in range(n_grp):
            rows = slice(k * grp, (k + 1) * grp)
            hs_ref[rows, lanes] = (a_ref[hd, rows, :] * c_ref[hd, k:k + 1, :]
                                   + b_ref[hd, rows, :])

    xg = rest_ref[:, 0:D_RNN]
    c0 = math.sqrt(2.0 / math.pi)
    gelu = 0.5 * xg * (1.0 + jnp.tanh(c0 * (xg + 0.044715 * (xg * xg * xg))))
    rp_ref[:, 0:D_RNN] = (gelu * hs_ref[...]).astype(_bf16)

    tile = jnp.minimum(s, pl.num_programs(1) - 2)
    pos = tile * tc + lax.broadcasted_iota(jnp.int32, (tc, HEAD_DIM), 0)
    for g, win in enumerate(POOL_WINDOWS):
        lo, hi = g * HEAD_DIM, (g + 1) * HEAD_DIM
        tot = xp_ext[:, lo:hi]
        span = 1
        while span < win:
            tot = tot + pltpu.roll(tot, span, 0)
            span *= 2
        cur = xp_ext[hist:, lo:hi]
        count = jnp.minimum(pos + 1, win).astype(_f32)
        dlt = (tot[hist:, :] / count - cur).astype(_bf16)
        y = jnp.dot(dlt, pool_w_ref[g], preferred_element_type=_f32)
        rp_ref[:, D_RNN + lo:D_RNN + hi] = (y * pool_scale_ref[:, lo:hi]).astype(_bf16)

    xr_ext[0:hist, :] = xr_ext[tc:tc + hist, :]
    xp_ext[0:hist, :] = xp_ext[tc:tc + hist, :]


def _mix_out(x, att, rest, w_out, layer, conv_w, conv_b, w_a, b_a, w_x, b_x, lam, pool_w,
             pool_scale, batch, seq):
    tc = SEQ_TILE
    n_grp = tc // SUBLANES
    n_tiles = seq // tc
    lagged = lambda width: pl.BlockSpec(
        (None, tc, width), lambda b, s: (b, jnp.maximum(s - 1, 0), 0))
    current = lambda width: pl.BlockSpec(
        (None, tc, width), lambda b, s: (b, jnp.minimum(s, n_tiles - 1), 0))
    vec = lambda n: pl.BlockSpec((1, n), lambda b, s: (0, 0))
    mats = pl.BlockSpec((RNN_HEADS, HEAD_DIM, HEAD_DIM), lambda b, s: (0, 0, 0))
    out = pl.pallas_call(
        _mix_out_kernel,
        grid=(batch, n_tiles + 1),
        in_specs=[
            lagged(D_MODEL), lagged(D_ATT), current(D_REST),
            pl.BlockSpec((None, D_MODEL, D_MODEL), lambda b, s: (layer, 0, 0),
                         pipeline_mode=pl.Buffered(1)),
            pl.BlockSpec((CONV_WIDTH, D_RNN), lambda b, s: (0, 0)),
            vec(D_RNN), mats, vec(D_RNN), mats, vec(D_RNN), vec(D_RNN),
            mats, vec(D_POOL),
        ],
        out_specs=lagged(D_MODEL),
        out_shape=jax.ShapeDtypeStruct((batch, seq, D_MODEL), _f32),
        scratch_shapes=[
            pltpu.VMEM((HISTORY_ROWS + tc, D_RNN), _f32),
            pltpu.VMEM((HISTORY_ROWS + tc, D_POOL), _f32),
            pltpu.VMEM((1, D_RNN), _f32),
            pltpu.VMEM((RNN_HEADS, tc, HEAD_DIM), _f32),
            pltpu.VMEM((RNN_HEADS, tc, HEAD_DIM), _f32),
            pltpu.VMEM((RNN_HEADS, n_grp, HEAD_DIM), _f32),
            pltpu.VMEM((tc, D_RNN), _f32),
            pltpu.VMEM((tc, D_RNN + D_POOL), _bf16),
        ],
        compiler_params=pltpu.CompilerParams(
            dimension_semantics=("parallel", "arbitrary"),
            vmem_limit_bytes=VMEM_LIMIT_BYTES),
        name="mix_out",
    )(x.reshape(batch, seq, D_MODEL), att.reshape(batch, seq, D_ATT),
      rest.reshape(batch, seq, D_REST), w_out, conv_w, conv_b, w_a, b_a, w_x, b_x, lam,
      pool_w, pool_scale)
    return out.reshape(batch * seq, D_MODEL)


def kernel(x, norm_ffn1, ffn1_gate, ffn1_up, ffn1_down, norm_mix, w_in, conv_w, conv_b, rg_w_a, rg_b_a, rg_w_x, rg_b_x, rg_lambda, pool_w, pool_scale, w_out, norm_ffn2, ffn2_gate, ffn2_up, ffn2_down, norm_final):
    batch, seq, _ = x.shape
    depth = w_in.shape[0]
    assert x.shape == (batch, seq, D_MODEL) and seq % max(2 * ATT_BLOCK, SEQ_TILE) == 0
    assert (batch * seq) % max(FFN_TOKEN_TILE, MIX_TOKEN_TILE) == 0
    xf = x.reshape(batch * seq, D_MODEL)
    row = lambda v: v.reshape(1, -1)
    final_gain = row(norm_final)

    w_in_b = _cast_bf16(w_in, W_IN_CAST_ROWS, "cast_w_in")
    w_out_b = _cast_bf16(w_out, W_OUT_CAST_ROWS, "cast_w_out")
    rg_w_a_b, rg_w_x_b, pool_w_b = (w.astype(_bf16) for w in (rg_w_a, rg_w_x, pool_w))

    wgu = _merge_gate_up(ffn1_gate, ffn1_up, 0)
    wd = _cast_layer_bf16(ffn1_down, 0, DOWN_CAST_ROWS, "cast_down")
    for l in range(depth):
        xf, wgu, wd = _ffn(xf, row(norm_ffn1[l]), wgu, wd, final_gain, False,
                           next_weights=(ffn2_gate, ffn2_up, ffn2_down, l))
        qkv, rest = _mix_in(xf, row(norm_mix[l]), w_in_b, l)
        att = _attention(qkv, batch, seq)
        xf = _mix_out(xf, att, rest, w_out_b, l, conv_w[l], row(conv_b[l]), rg_w_a_b[l],
                      row(rg_b_a[l]), rg_w_x_b[l], row(rg_b_x[l]), row(rg_lambda[l]),
                      pool_w_b[l], row(pool_scale[l]), batch, seq)
        if l + 1 < depth:
            xf, wgu, wd = _ffn(xf, row(norm_ffn2[l]), wgu, wd, final_gain, False,
                               next_weights=(ffn1_gate, ffn1_up, ffn1_down, l + 1))
        else:
            xf = _ffn(xf, row(norm_ffn2[l]), wgu, wd, final_gain, True)
    return xf.reshape(batch, seq, D_MODEL)
```

```python
import functools
import math

import jax
import jax.numpy as jnp
from jax import lax
from jax.experimental import pallas as pl
from jax.experimental.pallas import tpu as pltpu

D_MODEL = 2048
D_ATT = 1024
ATT_HEADS = 8
HEAD_DIM = 128
D_RNN = 512
RNN_HEADS = 4
D_POOL = 512
POOL_WINDOWS = (2, 4, 8, 16)
D_QKV = 3 * D_ATT
D_REST = 2 * D_RNN + D_POOL
D_IN_PROJ = D_QKV + D_REST
CONV_WIDTH = 4
RG_C = 8.0
D_FF = 5632
NORM_EPS = 1e-6

VMEM_LIMIT_BYTES = 60 * 1024 * 1024

FFN_TOKEN_TILE = 1024
FFN_HIDDEN_TILE = 512
MIX_TOKEN_TILE = 512
ATT_BLOCK = 256
ATT_HEADS_PER_STEP = 4
ATT_QBLOCKS_PER_STEP = 2
ATT_LOG2_CUTOFF = -100.0 * math.log2(math.e)
MASKED_LOGIT = 1e30
SEQ_TILE = 512
HISTORY_ROWS = 16
SUBLANES = 8
OUT_CHUNK = 512
DOWN_CAST_ROWS = 1408
W_IN_CAST_ROWS = 512
W_OUT_CAST_ROWS = 1024

_bf16 = jnp.bfloat16
_f32 = jnp.float32


def _rms_norm(x, gain):
    ms = jnp.mean(x * x, axis=-1, keepdims=True)
    return x * lax.rsqrt(ms + NORM_EPS) * gain


def _log_sigmoid(x):
    return jnp.minimum(x, 0.0) - jnp.log1p(jnp.exp(-jnp.abs(x)))


def _sigmoid(x):
    return 1.0 / (1.0 + jnp.exp(-x))


def _cast_kernel(w_ref, o_ref):
    o_ref[...] = w_ref[...].astype(_bf16)


def _cast_layer_bf16(w, layer, block_rows, name):
    _, rows, cols = w.shape
    return pl.pallas_call(
        _cast_kernel,
        grid=(rows // block_rows,),
        in_specs=[pl.BlockSpec((None, block_rows, cols), lambda r: (layer, r, 0))],
        out_specs=pl.BlockSpec((block_rows, cols), lambda r: (r, 0)),
        out_shape=jax.ShapeDtypeStruct((rows, cols), _bf16),
        compiler_params=pltpu.CompilerParams(
            dimension_semantics=("parallel",), vmem_limit_bytes=VMEM_LIMIT_BYTES),
        name=name,
    )(w)


def _cast_bf16(w, block_rows, name):
    depth, rows, cols = w.shape
    return pl.pallas_call(
        _cast_kernel,
        grid=(depth, rows // block_rows),
        in_specs=[pl.BlockSpec((None, block_rows, cols), lambda l, r: (l, r, 0))],
        out_specs=pl.BlockSpec((None, block_rows, cols), lambda l, r: (l, r, 0)),
        out_shape=jax.ShapeDtypeStruct(w.shape, _bf16),
        compiler_params=pltpu.CompilerParams(
            dimension_semantics=("parallel", "parallel"), vmem_limit_bytes=VMEM_LIMIT_BYTES),
        name=name,
    )(w)


def _merge_gate_up_kernel(g_ref, u_ref, o_ref):
    tf = g_ref.shape[-1]
    o_ref[:, :tf] = g_ref[...].astype(_bf16)
    o_ref[:, tf:] = u_ref[...].astype(_bf16)


def _merge_gate_up(gate, up, layer):
    tf = FFN_HIDDEN_TILE
    spec = pl.BlockSpec((None, D_MODEL, tf), lambda j: (layer, 0, j))
    return pl.pallas_call(
        _merge_gate_up_kernel,
        grid=(D_FF // tf,),
        in_specs=[spec, spec],
        out_specs=pl.BlockSpec((D_MODEL, 2 * tf), lambda j: (0, j)),
        out_shape=jax.ShapeDtypeStruct((D_MODEL, 2 * D_FF), _bf16),
        compiler_params=pltpu.CompilerParams(
            dimension_semantics=("parallel",), vmem_limit_bytes=VMEM_LIMIT_BYTES),
        name="merge_gate_up",
    )(gate, up)


def _ffn_kernel(x_ref, gain_ref, wgu_ref, wd_ref, fgain_ref, *rest, final_norm, cast_next):
    if cast_next:
        ng_ref, nu_ref, nd_ref, o_ref, ngu_ref, ndn_ref, h_ref = rest
        _merge_gate_up_kernel(ng_ref, nu_ref, ngu_ref)
        _cast_kernel(nd_ref, ndn_ref)
    else:
        o_ref, h_ref = rest
    j = pl.program_id(1)
    tf = FFN_HIDDEN_TILE

    def hidden_tile_contribution():
        gu = jnp.dot(h_ref[...], wgu_ref[...], preferred_element_type=_f32)
        g = gu[:, :tf]
        u = gu[:, tf:]
        a = (g * _sigmoid(g) * u).astype(_bf16)
        return jnp.dot(a, wd_ref[...], preferred_element_type=_f32)

    @pl.when(j == 0)
    def _():
        h_ref[...] = _rms_norm(x_ref[...], gain_ref[...]).astype(_bf16)
        o_ref[...] = hidden_tile_contribution()

    @pl.when(j > 0)
    def _():
        o_ref[...] += hidden_tile_contribution()

    @pl.when(j == pl.num_programs(1) - 1)
    def _():
        y = x_ref[...] + 0.5 * o_ref[...]
        if final_norm:
            y = _rms_norm(y, fgain_ref[...])
        o_ref[...] = y


def _ffn(x, gain, wgu, wd, final_gain, final_norm, next_weights=None):
    t = x.shape[0]
    tm, tf = FFN_TOKEN_TILE, FFN_HIDDEN_TILE
    n_tok = t // tm
    in_specs = [
        pl.BlockSpec((tm, D_MODEL), lambda i, j: (i, 0)),
        pl.BlockSpec((1, D_MODEL), lambda i, j: (0, 0)),
        pl.BlockSpec((D_MODEL, 2 * tf), lambda i, j: (0, j)),
        pl.BlockSpec((tf, D_MODEL), lambda i, j: (j, 0)),
        pl.BlockSpec((1, D_MODEL), lambda i, j: (0, 0)),
    ]
    out_specs = [pl.BlockSpec((tm, D_MODEL), lambda i, j: (i, 0))]
    out_shape = [jax.ShapeDtypeStruct((t, D_MODEL), _f32)]
    operands = [x, gain, wgu, wd, final_gain]
    if next_weights is not None:
        gate, up, down, nl = next_weights
        slab = D_MODEL // n_tok
        assert slab * n_tok == D_MODEL and slab % 128 == 0
        in_specs += [
            pl.BlockSpec((None, slab, tf), lambda i, j: (nl, i, j)),
            pl.BlockSpec((None, slab, tf), lambda i, j: (nl, i, j)),
            pl.BlockSpec((None, tf, slab), lambda i, j: (nl, j, i)),
        ]
        out_specs += [pl.BlockSpec((slab, 2 * tf), lambda i, j: (i, j)),
                      pl.BlockSpec((tf, slab), lambda i, j: (j, i))]
        out_shape += [jax.ShapeDtypeStruct((D_MODEL, 2 * D_FF), _bf16),
                      jax.ShapeDtypeStruct((D_FF, D_MODEL), _bf16)]
        operands += [gate, up, down]
    outs = pl.pallas_call(
        functools.partial(_ffn_kernel, final_norm=final_norm,
                          cast_next=next_weights is not None),
        grid=(n_tok, D_FF // tf),
        in_specs=in_specs,
        out_specs=out_specs,
        out_shape=out_shape,
        scratch_shapes=[pltpu.VMEM((tm, D_MODEL), _bf16)],
        compiler_params=pltpu.CompilerParams(
            dimension_semantics=("parallel", "arbitrary"),
            vmem_limit_bytes=VMEM_LIMIT_BYTES),
        name="ffn",
    )(*operands)
    return outs if next_weights is not None else outs[0]


def _mix_in_kernel(x_ref, gain_ref, w_ref, qkv_ref, rest_ref):
    h = _rms_norm(x_ref[...], gain_ref[...]).astype(_bf16)
    qkv_ref[...] = jnp.dot(h, w_ref[:, :D_QKV], preferred_element_type=_f32).astype(_bf16)
    rest_ref[...] = jnp.dot(h, w_ref[:, D_QKV:], preferred_element_type=_f32)


def _mix_in(x, gain, w_in, layer):
    t = x.shape[0]
    tm = MIX_TOKEN_TILE
    return pl.pallas_call(
        _mix_in_kernel,
        grid=(t // tm,),
        in_specs=[
            pl.BlockSpec((tm, D_MODEL), lambda i: (i, 0)),
            pl.BlockSpec((1, D_MODEL), lambda i: (0, 0)),
            pl.BlockSpec((None, D_MODEL, D_IN_PROJ), lambda i: (layer, 0, 0),
                         pipeline_mode=pl.Buffered(1)),
        ],
        out_specs=[
            pl.BlockSpec((tm, D_QKV), lambda i: (i, 0)),
            pl.BlockSpec((tm, D_REST), lambda i: (i, 0)),
        ],
        out_shape=[jax.ShapeDtypeStruct((t, D_QKV), _bf16),
                   jax.ShapeDtypeStruct((t, D_REST), _f32)],
        compiler_params=pltpu.CompilerParams(
            dimension_semantics=("parallel",), vmem_limit_bytes=VMEM_LIMIT_BYTES),
        name="mix_in",
    )(x, gain, w_in)


def _log2_fail(y):
    return -(jnp.maximum(y, 0.0) + jnp.log2(1.0 + jnp.exp2(-jnp.abs(y))))


def _split_bf16(x):
    hi = x.astype(_bf16)
    lo = (x - hi.astype(_f32)).astype(_bf16)
    return hi, lo


def _attention_kernel(q_ref, k_ref, v_ref, o_ref, acc_ref, carry_ref):
    i = pl.program_id(2)
    blk = ATT_BLOCK
    logit_scale = HEAD_DIM ** -0.5 * math.log2(math.e)
    row = lax.broadcasted_iota(jnp.int32, (blk, blk), 0)
    col = lax.broadcasted_iota(jnp.int32, (blk, blk), 1)
    upper = (row > col).astype(_bf16)
    contract_last = (((1,), (1,)), ((), ()))

    row2 = lax.broadcasted_iota(jnp.int32, (blk, 2 * blk), 0)
    col2 = lax.broadcasted_iota(jnp.int32, (blk, 2 * blk), 1)
    offset = col2 - row2

    chains = []
    for qb in range(ATT_QBLOCKS_PER_STEP):
        q_block = i * ATT_QBLOCKS_PER_STEP + qb
        first = jnp.maximum(q_block - 1, 0)
        start = pl.multiple_of(first * blk, blk)
        causal = offset < (q_block - first) * blk
        rows = slice(qb * blk, (qb + 1) * blk)
        for hd in range(ATT_HEADS_PER_STEP):
            lanes = slice(hd * HEAD_DIM, (hd + 1) * HEAD_DIM)
            chains.append((qb * ATT_HEADS_PER_STEP + hd, rows, lanes, first, start, causal))

    ys, lfs, css = [], [], []
    for _, rows, lanes, _, start, causal in chains:
        k2 = k_ref[pl.ds(start, 2 * blk), lanes]
        y = lax.dot_general(q_ref[rows, lanes], k2, contract_last,
                            preferred_element_type=_f32) * logit_scale
        ys.append(jnp.where(causal, y, -MASKED_LOGIT))
    for y in ys:
        lf = _log2_fail(y)
        hi, lo = _split_bf16(lf)
        stacked = jnp.concatenate([hi[:, blk:], lo[:, blk:], hi[:, :blk], lo[:, :blk]], axis=0)
        lfs.append(lf)
        css.append(jnp.dot(stacked, upper, preferred_element_type=_f32))
    for c, _, lanes, _, start, _ in chains:
        y, lf, cs = ys[c], lfs[c], css[c]
        after_r = cs[0:blk] + cs[blk:2 * blk]
        total_r = after_r[:, 0:1] + lf[:, blk:blk + 1]
        after_l = cs[2 * blk:3 * blk] + cs[3 * blk:] + total_r
        after = jnp.concatenate([after_l, after_r], axis=1)
        w = jnp.exp2(y + lf + after)
        v2 = v_ref[pl.ds(start, 2 * blk), lanes]
        acc_ref[c] = jnp.dot(w.astype(_bf16), v2, preferred_element_type=_f32)
        carry_ref[c] = after_l[:, 0:1] + lf[:, 0:1]

    def walk_earlier_blocks(c, rows, lanes, first):
        def cond(jb):
            return jnp.logical_and(jb >= 0, jnp.max(carry_ref[c]) > ATT_LOG2_CUTOFF)

        def body(jb):
            kstart = pl.multiple_of(jb * blk, blk)
            kb = k_ref[pl.ds(kstart, blk), lanes]
            vb = v_ref[pl.ds(kstart, blk), lanes]
            yb = lax.dot_general(q_ref[rows, lanes], kb, contract_last,
                                 preferred_element_type=_f32) * logit_scale
            lfb = _log2_fail(yb)
            hb, lb = _split_bf16(lfb)
            csb = jnp.dot(jnp.concatenate([hb, lb], axis=0), upper,
                          preferred_element_type=_f32)
            after_b = csb[0:blk] + csb[blk:] + carry_ref[c]
            wb = jnp.exp2(yb + lfb + after_b)
            acc_ref[c] += jnp.dot(wb.astype(_bf16), vb, preferred_element_type=_f32)
            carry_ref[c] = after_b[:, 0:1] + lfb[:, 0:1]
            return jb - 1

        lax.while_loop(cond, body, first - 1)

    hps = ATT_HEADS_PER_STEP
    any_walk = False
    for qb in range(ATT_QBLOCKS_PER_STEP):
        first = chains[qb * hps][3]
        open_mass = jnp.max(carry_ref[qb * hps:(qb + 1) * hps]) > ATT_LOG2_CUTOFF
        any_walk = jnp.logical_or(any_walk, jnp.logical_and(first >= 1, open_mass))

    @pl.when(any_walk)
    def _():
        for c, rows, lanes, first, _, _ in chains:
            walk_earlier_blocks(c, rows, lanes, first)

    for c, rows, lanes, _, _, _ in chains:
        o_ref[rows, lanes] = acc_ref[c].astype(o_ref.dtype)


def _attention(qkv, batch, seq):
    blk = ATT_BLOCK
    hps = ATT_HEADS_PER_STEP
    groups = ATT_HEADS // hps
    width = hps * HEAD_DIM
    q_rows = ATT_QBLOCKS_PER_STEP * blk
    n_chains = ATT_QBLOCKS_PER_STEP * hps
    qkv3 = qkv.reshape(batch, seq, D_QKV)
    out = pl.pallas_call(
        _attention_kernel,
        grid=(batch, groups, seq // q_rows),
        in_specs=[
            pl.BlockSpec((None, q_rows, width), lambda b, g, i: (b, i, g)),
            pl.BlockSpec((None, seq, width), lambda b, g, i: (b, 0, groups + g)),
            pl.BlockSpec((None, seq, width), lambda b, g, i: (b, 0, 2 * groups + g)),
        ],
        out_specs=pl.BlockSpec((None, q_rows, width), lambda b, g, i: (b, i, g)),
        out_shape=jax.ShapeDtypeStruct((batch, seq, D_ATT), _bf16),
        scratch_shapes=[pltpu.VMEM((n_chains, blk, HEAD_DIM), _f32),
                        pltpu.VMEM((n_chains, blk, 1), _f32)],
        compiler_params=pltpu.CompilerParams(
            dimension_semantics=("parallel", "parallel", "arbitrary"),
            vmem_limit_bytes=VMEM_LIMIT_BYTES),
        name="attention",
    )(qkv3, qkv3, qkv3)
    return out.reshape(batch * seq, D_ATT)


def _scan_rows(a, b, n):
    t_idx = lax.broadcasted_iota(jnp.int32, a.shape, 0)
    d = 1
    while d < n:
        keep = t_idx >= d
        a_sh = jnp.where(keep, pltpu.roll(a, d, 0), 1.0)
        b_sh = jnp.where(keep, pltpu.roll(b, d, 0), 0.0)
        b = a * b_sh + b
        a = a * a_sh
        d *= 2
    return a, b


def _mix_out_kernel(x_ref, att_ref, rest_ref, w_ref, conv_w_ref, conv_b_ref, wa_ref, ba_ref,
                    wx_ref, bx_ref, lam_ref, pool_w_ref, pool_scale_ref, o_ref,
                    xr_ext, xp_ext, h_ref, a_ref, b_ref, c_ref, hs_ref, rp_ref):
    s = pl.program_id(1)
    tc = SEQ_TILE
    hist = HISTORY_ROWS
    grp = SUBLANES
    n_grp = tc // grp

    @pl.when(s == 0)
    def _():
        xr_ext[0:hist, :] = jnp.zeros((hist, D_RNN), _f32)
        xp_ext[0:hist, :] = jnp.zeros((hist, D_POOL), _f32)
        h_ref[...] = jnp.zeros_like(h_ref)
        rp_ref[...] = jnp.zeros_like(rp_ref)

    def project_previous_tile(chunk):
        cols = slice(chunk * OUT_CHUNK, (chunk + 1) * OUT_CHUNK)
        y = jnp.dot(att_ref[...], w_ref[:D_ATT, cols], preferred_element_type=_f32)
        y = y + jnp.dot(rp_ref[...], w_ref[D_ATT:, cols], preferred_element_type=_f32)
        o_ref[:, cols] = x_ref[:, cols] + y

    project_previous_tile(0)
    xr_ext[hist:, :] = rest_ref[:, D_RNN:2 * D_RNN]
    xp_ext[hist:, :] = rest_ref[:, 2 * D_RNN:]

    u = jnp.zeros((tc, D_RNN), _f32) + conv_b_ref[...]
    for j in range(CONV_WIDTH):
        off = hist - (CONV_WIDTH - 1) + j
        u = u + conv_w_ref[j:j + 1, :] * xr_ext[off:off + tc, :]

    project_previous_tile(1)
    ub = u.astype(_bf16)
    r_parts, i_parts = [], []
    for hd in range(RNN_HEADS):
        uh = ub[:, hd * HEAD_DIM:(hd + 1) * HEAD_DIM]
        r_parts.append(jnp.dot(uh, wa_ref[hd], preferred_element_type=_f32))
        i_parts.append(jnp.dot(uh, wx_ref[hd], preferred_element_type=_f32))
    r = _sigmoid(jnp.concatenate(r_parts, axis=1) + ba_ref[...])
    ig = _sigmoid(jnp.concatenate(i_parts, axis=1) + bx_ref[...])
    log_a = RG_C * r * _log_sigmoid(lam_ref[...])
    a = jnp.exp(log_a)
    var = -jnp.tanh(log_a) * (1.0 + a * a)
    b = jnp.where(var > 0.0, var * lax.rsqrt(var), 0.0) * (ig * u)

    project_previous_tile(2)
    a3 = a.reshape(n_grp, grp, D_RNN)
    b3 = b.reshape(n_grp, grp, D_RNN)
    sub = lax.broadcasted_iota(jnp.int32, (n_grp, grp, D_RNN), 1)
    d = 1
    while d < grp:
        keep = sub >= d
        a_sh = jnp.where(keep, pltpu.roll(a3, d, 1), 1.0)
        b_sh = jnp.where(keep, pltpu.roll(b3, d, 1), 0.0)
        b3 = a3 * b_sh + b3
        a3 = a3 * a_sh
        d *= 2
    a = a3.reshape(tc, D_RNN)
    b = b3.reshape(tc, D_RNN)
    project_previous_tile(3)
    g_idx = lax.broadcasted_iota(jnp.int32, (n_grp, HEAD_DIM), 0)
    group_ends = pl.ds(grp - 1, n_grp, stride=grp)
    for hd in range(RNN_HEADS):
        lanes = slice(hd * HEAD_DIM, (hd + 1) * HEAD_DIM)
        a_ref[hd] = a[:, lanes]
        b_ref[hd] = b[:, lanes]
        a_end, b_end = _scan_rows(a_ref[hd, group_ends, :], b_ref[hd, group_ends, :], n_grp)
        h_in = h_ref[:, lanes]
        h_end = a_end * h_in + b_end
        c_ref[hd] = jnp.where(g_idx == 0, h_in, pltpu.roll(h_end, 1, 0))
        h_ref[:, lanes] = h_end[n_grp - 1:n_grp, :]
        for k in range(n_grp):
            rows = slice(k * grp, (k + 1) * grp)
            hs_ref[rows, lanes] = (a_ref[hd, rows, :] * c_ref[hd, k:k + 1, :]
                                   + b_ref[hd, rows, :])

    xg = rest_ref[:, 0:D_RNN]
    c0 = math.sqrt(2.0 / math.pi)
    gelu = 0.5 * xg * (1.0 + jnp.tanh(c0 * (xg + 0.044715 * (xg * xg * xg))))
    rp_ref[:, 0:D_RNN] = (gelu * hs_ref[...]).astype(_bf16)

    tile = jnp.minimum(s, pl.num_programs(1) - 2)
    pos = tile * tc + lax.broadcasted_iota(jnp.int32, (tc, HEAD_DIM), 0)
    for g, win in enumerate(POOL_WINDOWS):
        lo, hi = g * HEAD_DIM, (g + 1) * HEAD_DIM
        tot = xp_ext[:, lo:hi]
        span = 1
        while span < win:
            tot = tot + pltpu.roll(tot, span, 0)
            span *= 2
        cur = xp_ext[hist:, lo:hi]
        count = jnp.minimum(pos + 1, win).astype(_f32)
        dlt = (tot[hist:, :] / count - cur).astype(_bf16)
        y = jnp.dot(dlt, pool_w_ref[g], preferred_element_type=_f32)
        rp_ref[:, D_RNN + lo:D_RNN + hi] = (y * pool_scale_ref[:, lo:hi]).astype(_bf16)

    xr_ext[0:hist, :] = xr_ext[tc:tc + hist, :]
    xp_ext[0:hist, :] = xp_ext[tc:tc + hist, :]


def _mix_out(x, att, rest, w_out, layer, conv_w, conv_b, w_a, b_a, w_x, b_x, lam, pool_w,
             pool_scale, batch, seq):
    tc = SEQ_TILE
    n_grp = tc // SUBLANES
    n_tiles = seq // tc
    lagged = lambda width: pl.BlockSpec(
        (None, tc, width), lambda b, s: (b, jnp.maximum(s - 1, 0), 0))
    current = lambda width: pl.BlockSpec(
        (None, tc, width), lambda b, s: (b, jnp.minimum(s, n_tiles - 1), 0))
    vec = lambda n: pl.BlockSpec((1, n), lambda b, s: (0, 0))
    mats = pl.BlockSpec((RNN_HEADS, HEAD_DIM, HEAD_DIM), lambda b, s: (0, 0, 0))
    out = pl.pallas_call(
        _mix_out_kernel,
        grid=(batch, n_tiles + 1),
        in_specs=[
            lagged(D_MODEL), lagged(D_ATT), current(D_REST),
            pl.BlockSpec((None, D_MODEL, D_MODEL), lambda b, s: (layer, 0, 0),
                         pipeline_mode=pl.Buffered(1)),
            pl.BlockSpec((CONV_WIDTH, D_RNN), lambda b, s: (0, 0)),
            vec(D_RNN), mats, vec(D_RNN), mats, vec(D_RNN), vec(D_RNN),
            mats, vec(D_POOL),
        ],
        out_specs=lagged(D_MODEL),
        out_shape=jax.ShapeDtypeStruct((batch, seq, D_MODEL), _f32),
        scratch_shapes=[
            pltpu.VMEM((HISTORY_ROWS + tc, D_RNN), _f32),
            pltpu.VMEM((HISTORY_ROWS + tc, D_POOL), _f32),
            pltpu.VMEM((1, D_RNN), _f32),
            pltpu.VMEM((RNN_HEADS, tc, HEAD_DIM), _f32),
            pltpu.VMEM((RNN_HEADS, tc, HEAD_DIM), _f32),
            pltpu.VMEM((RNN_HEADS, n_grp, HEAD_DIM), _f32),
            pltpu.VMEM((tc, D_RNN), _f32),
            pltpu.VMEM((tc, D_RNN + D_POOL), _bf16),
        ],
        compiler_params=pltpu.CompilerParams(
            dimension_semantics=("parallel", "arbitrary"),
            vmem_limit_bytes=VMEM_LIMIT_BYTES),
        name="mix_out",
    )(x.reshape(batch, seq, D_MODEL), att.reshape(batch, seq, D_ATT),
      rest.reshape(batch, seq, D_REST), w_out, conv_w, conv_b, w_a, b_a, w_x, b_x, lam,
      pool_w, pool_scale)
    return out.reshape(batch * seq, D_MODEL)


def kernel(x, norm_ffn1, ffn1_gate, ffn1_up, ffn1_down, norm_mix, w_in, conv_w, conv_b, rg_w_a, rg_b_a, rg_w_x, rg_b_x, rg_lambda, pool_w, pool_scale, w_out, norm_ffn2, ffn2_gate, ffn2_up, ffn2_down, norm_final):
    batch, seq, _ = x.shape
    depth = w_in.shape[0]
    assert x.shape == (batch, seq, D_MODEL) and seq % max(2 * ATT_BLOCK, SEQ_TILE) == 0
    assert (batch * seq) % max(FFN_TOKEN_TILE, MIX_TOKEN_TILE) == 0
    xf = x.reshape(batch * seq, D_MODEL)
    row = lambda v: v.reshape(1, -1)
    final_gain = row(norm_final)

    w_in_b = _cast_bf16(w_in, W_IN_CAST_ROWS, "cast_w_in")
    w_out_b = _cast_bf16(w_out, W_OUT_CAST_ROWS, "cast_w_out")
    rg_w_a_b, rg_w_x_b, pool_w_b = (w.astype(_bf16) for w in (rg_w_a, rg_w_x, pool_w))

    wgu = _merge_gate_up(ffn1_gate, ffn1_up, 0)
    wd = _cast_layer_bf16(ffn1_down, 0, DOWN_CAST_ROWS, "cast_down")
    for l in range(depth):
        xf, wgu, wd = _ffn(xf, row(norm_ffn1[l]), wgu, wd, final_gain, False,
                           next_weights=(ffn2_gate, ffn2_up, ffn2_down, l))
        qkv, rest = _mix_in(xf, row(norm_mix[l]), w_in_b, l)
        att = _attention(qkv, batch, seq)
        xf = _mix_out(xf, att, rest, w_out_b, l, conv_w[l], row(conv_b[l]), rg_w_a_b[l],
                      row(rg_b_a[l]), rg_w_x_b[l], row(rg_b_x[l]), row(rg_lambda[l]),
                      pool_w_b[l], row(pool_scale[l]), batch, seq)
        if l + 1 < depth:
            xf, wgu, wd = _ffn(xf, row(norm_ffn2[l]), wgu, wd, final_gain, False,
                               next_weights=(ffn1_gate, ffn1_up, ffn1_down, l + 1))
        else:
            xf = _ffn(xf, row(norm_ffn2[l]), wgu, wd, final_gain, True)
    return xf.reshape(batch, seq, D_MODEL)
```

```python
import functools
import math

import jax
import jax.numpy as jnp
from jax import lax
from jax.experimental import pallas as pl
from jax.experimental.pallas import tpu as pltpu

D_MODEL = 2048
D_ATT = 1024
ATT_HEADS = 8
HEAD_DIM = 128
D_RNN = 512
RNN_HEADS = 4
D_POOL = 512
POOL_WINDOWS = (2, 4, 8, 16)
D_QKV = 3 * D_ATT
D_REST = 2 * D_RNN + D_POOL
D_IN_PROJ = D_QKV + D_REST
CONV_WIDTH = 4
RG_C = 8.0
D_FF = 5632
NORM_EPS = 1e-6

VMEM_LIMIT_BYTES = 60 * 1024 * 1024

FFN_TOKEN_TILE = 1024
FFN_HIDDEN_TILE = 512
MIX_TOKEN_TILE = 512
ATT_BLOCK = 256
ATT_HEADS_PER_STEP = 4
ATT_QBLOCKS_PER_STEP = 2
ATT_LOG2_CUTOFF = -100.0 * math.log2(math.e)
MASKED_LOGIT = 1e30
SEQ_TILE = 512
HISTORY_ROWS = 16
SUBLANES = 8
OUT_CHUNK = 512
DOWN_CAST_ROWS = 1408
W_IN_CAST_ROWS = 512
W_OUT_CAST_ROWS = 1024

_bf16 = jnp.bfloat16
_f32 = jnp.float32


def _rms_norm(x, gain):
    ms = jnp.mean(x * x, axis=-1, keepdims=True)
    return x * lax.rsqrt(ms + NORM_EPS) * gain


def _log_sigmoid(x):
    return jnp.minimum(x, 0.0) - jnp.log1p(jnp.exp(-jnp.abs(x)))


def _sigmoid(x):
    return 1.0 / (1.0 + jnp.exp(-x))


def _cast_kernel(w_ref, o_ref):
    o_ref[...] = w_ref[...].astype(_bf16)


def _cast_layer_bf16(w, layer, block_rows, name):
    _, rows, cols = w.shape
    return pl.pallas_call(
        _cast_kernel,
        grid=(rows // block_rows,),
        in_specs=[pl.BlockSpec((None, block_rows, cols), lambda r: (layer, r, 0))],
        out_specs=pl.BlockSpec((block_rows, cols), lambda r: (r, 0)),
        out_shape=jax.ShapeDtypeStruct((rows, cols), _bf16),
        compiler_params=pltpu.CompilerParams(
            dimension_semantics=("parallel",), vmem_limit_bytes=VMEM_LIMIT_BYTES),
        name=name,
    )(w)


def _cast_bf16(w, block_rows, name):
    depth, rows, cols = w.shape
    return pl.pallas_call(
        _cast_kernel,
        grid=(depth, rows // block_rows),
        in_specs=[pl.BlockSpec((None, block_rows, cols), lambda l, r: (l, r, 0))],
        out_specs=pl.BlockSpec((None, block_rows, cols), lambda l, r: (l, r, 0)),
        out_shape=jax.ShapeDtypeStruct(w.shape, _bf16),
        compiler_params=pltpu.CompilerParams(
            dimension_semantics=("parallel", "parallel"), vmem_limit_bytes=VMEM_LIMIT_BYTES),
        name=name,
    )(w)


def _merge_gate_up_kernel(g_ref, u_ref, o_ref):
    tf = g_ref.shape[-1]
    o_ref[:, :tf] = g_ref[...].astype(_bf16)
    o_ref[:, tf:] = u_ref[...].astype(_bf16)


def _merge_gate_up(gate, up, layer):
    tf = FFN_HIDDEN_TILE
    spec = pl.BlockSpec((None, D_MODEL, tf), lambda j: (layer, 0, j))
    return pl.pallas_call(
        _merge_gate_up_kernel,
        grid=(D_FF // tf,),
        in_specs=[spec, spec],
        out_specs=pl.BlockSpec((D_MODEL, 2 * tf), lambda j: (0, j)),
        out_shape=jax.ShapeDtypeStruct((D_MODEL, 2 * D_FF), _bf16),
        compiler_params=pltpu.CompilerParams(
            dimension_semantics=("parallel",), vmem_limit_bytes=VMEM_LIMIT_BYTES),
        name="merge_gate_up",
    )(gate, up)


def _ffn_kernel(x_ref, gain_ref, wgu_ref, wd_ref, fgain_ref, *rest, final_norm, cast_next):
    if cast_next:
        ng_ref, nu_ref, nd_ref, o_ref, ngu_ref, ndn_ref, h_ref = rest
        _merge_gate_up_kernel(ng_ref, nu_ref, ngu_ref)
        _cast_kernel(nd_ref, ndn_ref)
    else:
        o_ref, h_ref = rest
    j = pl.program_id(1)
    tf = FFN_HIDDEN_TILE

    def hidden_tile_contribution():
        gu = jnp.dot(h_ref[...], wgu_ref[...], preferred_element_type=_f32)
        g = gu[:, :tf]
        u = gu[:, tf:]
        a = (g * _sigmoid(g) * u).astype(_bf16)
        return jnp.dot(a, wd_ref[...], preferred_element_type=_f32)

    @pl.when(j == 0)
    def _():
        h_ref[...] = _rms_norm(x_ref[...], gain_ref[...]).astype(_bf16)
        o_ref[...] = hidden_tile_contribution()

    last = pl.num_programs(1) - 1

    @pl.when(jnp.logical_and(j > 0, j < last))
    def _():
        o_ref[...] += hidden_tile_contribution()

    @pl.when(j == last)
    def _():
        if final_norm:
            o_ref[...] += hidden_tile_contribution()
            o_ref[...] = _rms_norm(x_ref[...] + 0.5 * o_ref[...], fgain_ref[...])
        else:
            o_ref[...] = x_ref[...] + 0.5 * (o_ref[...] + hidden_tile_contribution())


def _ffn(x, gain, wgu, wd, final_gain, final_norm, next_weights=None):
    t = x.shape[0]
    tm, tf = FFN_TOKEN_TILE, FFN_HIDDEN_TILE
    n_tok = t // tm
    in_specs = [
        pl.BlockSpec((tm, D_MODEL), lambda i, j: (i, 0)),
        pl.BlockSpec((1, D_MODEL), lambda i, j: (0, 0)),
        pl.BlockSpec((D_MODEL, 2 * tf), lambda i, j: (0, j)),
        pl.BlockSpec((tf, D_MODEL), lambda i, j: (j, 0)),
        pl.BlockSpec((1, D_MODEL), lambda i, j: (0, 0)),
    ]
    out_specs = [pl.BlockSpec((tm, D_MODEL), lambda i, j: (i, 0))]
    out_shape = [jax.ShapeDtypeStruct((t, D_MODEL), _f32)]
    operands = [x, gain, wgu, wd, final_gain]
    if next_weights is not None:
        gate, up, down, nl = next_weights
        slab = D_MODEL // n_tok
        assert slab * n_tok == D_MODEL and slab % 128 == 0
        in_specs += [
            pl.BlockSpec((None, slab, tf), lambda i, j: (nl, i, j)),
            pl.BlockSpec((None, slab, tf), lambda i, j: (nl, i, j)),
            pl.BlockSpec((None, tf, slab), lambda i, j: (nl, j, i)),
        ]
        out_specs += [pl.BlockSpec((slab, 2 * tf), lambda i, j: (i, j)),
                      pl.BlockSpec((tf, slab), lambda i, j: (j, i))]
        out_shape += [jax.ShapeDtypeStruct((D_MODEL, 2 * D_FF), _bf16),
                      jax.ShapeDtypeStruct((D_FF, D_MODEL), _bf16)]
        operands += [gate, up, down]
    outs = pl.pallas_call(
        functools.partial(_ffn_kernel, final_norm=final_norm,
                          cast_next=next_weights is not None),
        grid=(n_tok, D_FF // tf),
        in_specs=in_specs,
        out_specs=out_specs,
        out_shape=out_shape,
        scratch_shapes=[pltpu.VMEM((tm, D_MODEL), _bf16)],
        compiler_params=pltpu.CompilerParams(
            dimension_semantics=("parallel", "arbitrary"),
            vmem_limit_bytes=VMEM_LIMIT_BYTES),
        name="ffn",
    )(*operands)
    return outs if next_weights is not None else outs[0]


def _mix_in_kernel(x_ref, gain_ref, w_ref, qkv_ref, rest_ref):
    h = _rms_norm(x_ref[...], gain_ref[...]).astype(_bf16)
    qkv_ref[...] = jnp.dot(h, w_ref[:, :D_QKV], preferred_element_type=_f32).astype(_bf16)
    rest_ref[...] = jnp.dot(h, w_ref[:, D_QKV:], preferred_element_type=_f32)


def _mix_in(x, gain, w_in, layer):
    t = x.shape[0]
    tm = MIX_TOKEN_TILE
    return pl.pallas_call(
        _mix_in_kernel,
        grid=(t // tm,),
        in_specs=[
            pl.BlockSpec((tm, D_MODEL), lambda i: (i, 0)),
            pl.BlockSpec((1, D_MODEL), lambda i: (0, 0)),
            pl.BlockSpec((None, D_MODEL, D_IN_PROJ), lambda i: (layer, 0, 0),
                         pipeline_mode=pl.Buffered(1)),
        ],
        out_specs=[
            pl.BlockSpec((tm, D_QKV), lambda i: (i, 0)),
            pl.BlockSpec((tm, D_REST), lambda i: (i, 0)),
        ],
        out_shape=[jax.ShapeDtypeStruct((t, D_QKV), _bf16),
                   jax.ShapeDtypeStruct((t, D_REST), _f32)],
        compiler_params=pltpu.CompilerParams(
            dimension_semantics=("parallel",), vmem_limit_bytes=VMEM_LIMIT_BYTES),
        name="mix_in",
    )(x, gain, w_in)


def _log2_fail(y):
    return -(jnp.maximum(y, 0.0) + jnp.log2(1.0 + jnp.exp2(-jnp.abs(y))))


def _split_bf16(x):
    hi = x.astype(_bf16)
    lo = (x - hi.astype(_f32)).astype(_bf16)
    return hi, lo


def _attention_kernel(q_ref, k_ref, v_ref, o_ref, acc_ref, carry_ref):
    i = pl.program_id(2)
    blk = ATT_BLOCK
    logit_scale = HEAD_DIM ** -0.5 * math.log2(math.e)
    row = lax.broadcasted_iota(jnp.int32, (blk, blk), 0)
    col = lax.broadcasted_iota(jnp.int32, (blk, blk), 1)
    upper = (row > col).astype(_bf16)
    contract_last = (((1,), (1,)), ((), ()))

    row2 = lax.broadcasted_iota(jnp.int32, (blk, 2 * blk), 0)
    col2 = lax.broadcasted_iota(jnp.int32, (blk, 2 * blk), 1)
    offset = col2 - row2

    chains = []
    for qb in range(ATT_QBLOCKS_PER_STEP):
        q_block = i * ATT_QBLOCKS_PER_STEP + qb
        first = jnp.maximum(q_block - 1, 0)
        start = pl.multiple_of(first * blk, blk)
        causal = offset < (q_block - first) * blk
        rows = slice(qb * blk, (qb + 1) * blk)
        for hd in range(ATT_HEADS_PER_STEP):
            lanes = slice(hd * HEAD_DIM, (hd + 1) * HEAD_DIM)
            chains.append((qb * ATT_HEADS_PER_STEP + hd, rows, lanes, first, start, causal))

    ys, lfs, css = [], [], []
    for _, rows, lanes, _, start, causal in chains:
        k2 = k_ref[pl.ds(start, 2 * blk), lanes]
        y = lax.dot_general(q_ref[rows, lanes], k2, contract_last,
                            preferred_element_type=_f32) * logit_scale
        ys.append(jnp.where(causal, y, -MASKED_LOGIT))
    for y in ys:
        lf = _log2_fail(y)
        hi, lo = _split_bf16(lf)
        stacked = jnp.concatenate([hi[:, blk:], lo[:, blk:], hi[:, :blk], lo[:, :blk]], axis=0)
        lfs.append(lf)
        css.append(jnp.dot(stacked, upper, preferred_element_type=_f32))
    for c, _, lanes, _, start, _ in chains:
        y, lf, cs = ys[c], lfs[c], css[c]
        after_r = cs[0:blk] + cs[blk:2 * blk]
        total_r = after_r[:, 0:1] + lf[:, blk:blk + 1]
        after_l = cs[2 * blk:3 * blk] + cs[3 * blk:] + total_r
        after = jnp.concatenate([after_l, after_r], axis=1)
        w = jnp.exp2(y + lf + after)
        v2 = v_ref[pl.ds(start, 2 * blk), lanes]
        acc_ref[c] = jnp.dot(w.astype(_bf16), v2, preferred_element_type=_f32)
        carry_ref[c] = after_l[:, 0:1] + lf[:, 0:1]

    def walk_earlier_blocks(c, rows, lanes, first):
        def cond(jb):
            return jnp.logical_and(jb >= 0, jnp.max(carry_ref[c]) > ATT_LOG2_CUTOFF)

        def body(jb):
            kstart = pl.multiple_of(jb * blk, blk)
            kb = k_ref[pl.ds(kstart, blk), lanes]
            vb = v_ref[pl.ds(kstart, blk), lanes]
            yb = lax.dot_general(q_ref[rows, lanes], kb, contract_last,
                                 preferred_element_type=_f32) * logit_scale
            lfb = _log2_fail(yb)
            hb, lb = _split_bf16(lfb)
            csb = jnp.dot(jnp.concatenate([hb, lb], axis=0), upper,
                          preferred_element_type=_f32)
            after_b = csb[0:blk] + csb[blk:] + carry_ref[c]
            wb = jnp.exp2(yb + lfb + after_b)
            acc_ref[c] += jnp.dot(wb.astype(_bf16), vb, preferred_element_type=_f32)
            carry_ref[c] = after_b[:, 0:1] + lfb[:, 0:1]
            return jb - 1

        lax.while_loop(cond, body, first - 1)

    hps = ATT_HEADS_PER_STEP
    any_walk = False
    for qb in range(ATT_QBLOCKS_PER_STEP):
        first = chains[qb * hps][3]
        open_mass = jnp.max(carry_ref[qb * hps:(qb + 1) * hps]) > ATT_LOG2_CUTOFF
        any_walk = jnp.logical_or(any_walk, jnp.logical_and(first >= 1, open_mass))

    @pl.when(any_walk)
    def _():
        for c, rows, lanes, first, _, _ in chains:
            walk_earlier_blocks(c, rows, lanes, first)

    for c, rows, lanes, _, _, _ in chains:
        o_ref[rows, lanes] = acc_ref[c].astype(o_ref.dtype)


def _attention(qkv, batch, seq):
    blk = ATT_BLOCK
    hps = ATT_HEADS_PER_STEP
    groups = ATT_HEADS // hps
    width = hps * HEAD_DIM
    q_rows = ATT_QBLOCKS_PER_STEP * blk
    n_chains = ATT_QBLOCKS_PER_STEP * hps
    qkv3 = qkv.reshape(batch, seq, D_QKV)
    out = pl.pallas_call(
        _attention_kernel,
        grid=(batch, groups, seq // q_rows),
        in_specs=[
            pl.BlockSpec((None, q_rows, width), lambda b, g, i: (b, i, g)),
            pl.BlockSpec((None, seq, width), lambda b, g, i: (b, 0, groups + g)),
            pl.BlockSpec((None, seq, width), lambda b, g, i: (b, 0, 2 * groups + g)),
        ],
        out_specs=pl.BlockSpec((None, q_rows, width), lambda b, g, i: (b, i, g)),
        out_shape=jax.ShapeDtypeStruct((batch, seq, D_ATT), _bf16),
        scratch_shapes=[pltpu.VMEM((n_chains, blk, HEAD_DIM), _f32),
                        pltpu.VMEM((n_chains, blk, 1), _f32)],
        compiler_params=pltpu.CompilerParams(
            dimension_semantics=("parallel", "parallel", "arbitrary"),
            vmem_limit_bytes=VMEM_LIMIT_BYTES),
        name="attention",
    )(qkv3, qkv3, qkv3)
    return out.reshape(batch * seq, D_ATT)


def _scan_rows(a, b, n):
    t_idx = lax.broadcasted_iota(jnp.int32, a.shape, 0)
    d = 1
    while d < n:
        keep = t_idx >= d
        a_sh = jnp.where(keep, pltpu.roll(a, d, 0), 1.0)
        b_sh = jnp.where(keep, pltpu.roll(b, d, 0), 0.0)
        b = a * b_sh + b
        a = a * a_sh
        d *= 2
    return a, b


def _mix_out_kernel(x_ref, att_ref, rest_ref, w_ref, conv_w_ref, conv_b_ref, wa_ref, ba_ref,
                    wx_ref, bx_ref, lam_ref, pool_w_ref, pool_scale_ref, o_ref,
                    xr_ext, xp_ext, h_ref, a_ref, b_ref, c_ref, hs_ref, rp_ref):
    s = pl.program_id(1)
    tc = SEQ_TILE
    hist = HISTORY_ROWS
    grp = SUBLANES
    n_grp = tc // grp

    @pl.when(s == 0)
    def _():
        xr_ext[0:hist, :] = jnp.zeros((hist, D_RNN), _f32)
        xp_ext[0:hist, :] = jnp.zeros((hist, D_POOL), _f32)
        h_ref[...] = jnp.zeros_like(h_ref)
        rp_ref[...] = jnp.zeros_like(rp_ref)

    def project_previous_tile(chunk):
        cols = slice(chunk * OUT_CHUNK, (chunk + 1) * OUT_CHUNK)
        y = jnp.dot(att_ref[...], w_ref[:D_ATT, cols], preferred_element_type=_f32)
        y = y + jnp.dot(rp_ref[...], w_ref[D_ATT:, cols], preferred_element_type=_f32)
        o_ref[:, cols] = x_ref[:, cols] + y

    project_previous_tile(0)
    xr_ext[hist:, :] = rest_ref[:, D_RNN:2 * D_RNN]
    xp_ext[hist:, :] = rest_ref[:, 2 * D_RNN:]

    xr_all = xr_ext[...]
    u = conv_b_ref[...] + conv_w_ref[CONV_WIDTH - 1:CONV_WIDTH, :] * xr_all[hist:, :]
    for back in range(1, CONV_WIDTH):
        j = CONV_WIDTH - 1 - back
        u = u + conv_w_ref[j:j + 1, :] * pltpu.roll(xr_all, back, 0)[hist:, :]

    project_previous_tile(1)
    ub = u.astype(_bf16)
    r_parts, i_parts = [], []
    for hd in range(RNN_HEADS):
        uh = ub[:, hd * HEAD_DIM:(hd + 1) * HEAD_DIM]
        r_parts.append(jnp.dot(uh, wa_ref[hd], preferred_element_type=_f32))
        i_parts.append(jnp.dot(uh, wx_ref[hd], preferred_element_type=_f32))
    r = _sigmoid(jnp.concatenate(r_parts, axis=1) + ba_ref[...])
    ig = _sigmoid(jnp.concatenate(i_parts, axis=1) + bx_ref[...])
    log_a = RG_C * r * _log_sigmoid(lam_ref[...])
    a = jnp.exp(log_a)
    var = -jnp.tanh(log_a) * (1.0 + a * a)
    b = jnp.where(var > 0.0, var * lax.rsqrt(var), 0.0) * (ig * u)

    project_previous_tile(2)
    a3 = a.reshape(n_grp, grp, D_RNN)
    b3 = b.reshape(n_grp, grp, D_RNN)
    sub = lax.broadcasted_iota(jnp.int32, (n_grp, grp, D_RNN), 1)
    d = 1
    while d < grp:
        keep = sub >= d
        a_sh = jnp.where(keep, pltpu.roll(a3, d, 1), 1.0)
        b_sh = jnp.where(keep, pltpu.roll(b3, d, 1), 0.0)
        b3 = a3 * b_sh + b3
        a3 = a3 * a_sh
        d *= 2
    a = a3.reshape(tc, D_RNN)
    b = b3.reshape(tc, D_RNN)
    project_previous_tile(3)
    g_idx = lax.broadcasted_iota(jnp.int32, (n_grp, HEAD_DIM), 0)
    group_ends = pl.ds(grp - 1, n_grp, stride=grp)
    for hd in range(RNN_HEADS):
        lanes = slice(hd * HEAD_DIM, (hd + 1) * HEAD_DIM)
        a_ref[hd] = a[:, lanes]
        b_ref[hd] = b[:, lanes]
        a_end, b_end = _scan_rows(a_ref[hd, group_ends, :], b_ref[hd, group_ends, :], n_grp)
        h_in = h_ref[:, lanes]
        h_end = a_end * h_in + b_end
        c_ref[hd] = jnp.where(g_idx == 0, h_in, pltpu.roll(h_end, 1, 0))
        h_ref[:, lanes] = h_end[n_grp - 1:n_grp, :]
        for k in range(n_grp):
            rows = slice(k * grp, (k + 1) * grp)
            hs_ref[rows, lanes] = (a_ref[hd, rows, :] * c_ref[hd, k:k + 1, :]
                                   + b_ref[hd, rows, :])

    xg = rest_ref[:, 0:D_RNN]
    c0 = math.sqrt(2.0 / math.pi)
    gelu = 0.5 * xg * (1.0 + jnp.tanh(c0 * (xg + 0.044715 * (xg * xg * xg))))
    rp_ref[:, 0:D_RNN] = (gelu * hs_ref[...]).astype(_bf16)

    tile = jnp.minimum(s, pl.num_programs(1) - 2)
    pos = tile * tc + lax.broadcasted_iota(jnp.int32, (tc, HEAD_DIM), 0)
    for g, win in enumerate(POOL_WINDOWS):
        lo, hi = g * HEAD_DIM, (g + 1) * HEAD_DIM
        tot = xp_ext[:, lo:hi]
        span = 1
        while span < win:
            tot = tot + pltpu.roll(tot, span, 0)
            span *= 2
        cur = xp_ext[hist:, lo:hi]
        count = jnp.minimum(pos + 1, win).astype(_f32)
        dlt = (tot[hist:, :] / count - cur).astype(_bf16)
        y = jnp.dot(dlt, pool_w_ref[g], preferred_element_type=_f32)
        rp_ref[:, D_RNN + lo:D_RNN + hi] = (y * pool_scale_ref[:, lo:hi]).astype(_bf16)

    xr_ext[0:hist, :] = xr_ext[tc:tc + hist, :]
    xp_ext[0:hist, :] = xp_ext[tc:tc + hist, :]


def _mix_out(x, att, rest, w_out, layer, conv_w, conv_b, w_a, b_a, w_x, b_x, lam, pool_w,
             pool_scale, batch, seq):
    tc = SEQ_TILE
    n_grp = tc // SUBLANES
    n_tiles = seq // tc
    lagged = lambda width: pl.BlockSpec(
        (None, tc, width), lambda b, s: (b, jnp.maximum(s - 1, 0), 0))
    current = lambda width: pl.BlockSpec(
        (None, tc, width), lambda b, s: (b, jnp.minimum(s, n_tiles - 1), 0))
    vec = lambda n: pl.BlockSpec((1, n), lambda b, s: (0, 0))
    mats = pl.BlockSpec((RNN_HEADS, HEAD_DIM, HEAD_DIM), lambda b, s: (0, 0, 0))
    out = pl.pallas_call(
        _mix_out_kernel,
        grid=(batch, n_tiles + 1),
        in_specs=[
            lagged(D_MODEL), lagged(D_ATT), current(D_REST),
            pl.BlockSpec((None, D_MODEL, D_MODEL), lambda b, s: (layer, 0, 0),
                         pipeline_mode=pl.Buffered(1)),
            pl.BlockSpec((CONV_WIDTH, D_RNN), lambda b, s: (0, 0)),
            vec(D_RNN), mats, vec(D_RNN), mats, vec(D_RNN), vec(D_RNN),
            mats, vec(D_POOL),
        ],
        out_specs=lagged(D_MODEL),
        out_shape=jax.ShapeDtypeStruct((batch, seq, D_MODEL), _f32),
        scratch_shapes=[
            pltpu.VMEM((HISTORY_ROWS + tc, D_RNN), _f32),
            pltpu.VMEM((HISTORY_ROWS + tc, D_POOL), _f32),
            pltpu.VMEM((1, D_RNN), _f32),
            pltpu.VMEM((RNN_HEADS, tc, HEAD_DIM), _f32),
            pltpu.VMEM((RNN_HEADS, tc, HEAD_DIM), _f32),
            pltpu.VMEM((RNN_HEADS, n_grp, HEAD_DIM), _f32),
            pltpu.VMEM((tc, D_RNN), _f32),
            pltpu.VMEM((tc, D_RNN + D_POOL), _bf16),
        ],
        compiler_params=pltpu.CompilerParams(
            dimension_semantics=("parallel", "arbitrary"),
            vmem_limit_bytes=VMEM_LIMIT_BYTES),
        name="mix_out",
    )(x.reshape(batch, seq, D_MODEL), att.reshape(batch, seq, D_ATT),
      rest.reshape(batch, seq, D_REST), w_out, conv_w, conv_b, w_a, b_a, w_x, b_x, lam,
      pool_w, pool_scale)
    return out.reshape(batch * seq, D_MODEL)


def kernel(x, norm_ffn1, ffn1_gate, ffn1_up, ffn1_down, norm_mix, w_in, conv_w, conv_b, rg_w_a, rg_b_a, rg_w_x, rg_b_x, rg_lambda, pool_w, pool_scale, w_out, norm_ffn2, ffn2_gate, ffn2_up, ffn2_down, norm_final):
    batch, seq, _ = x.shape
    depth = w_in.shape[0]
    assert x.shape == (batch, seq, D_MODEL) and seq % max(2 * ATT_BLOCK, SEQ_TILE) == 0
    assert (batch * seq) % max(FFN_TOKEN_TILE, MIX_TOKEN_TILE) == 0
    xf = x.reshape(batch * seq, D_MODEL)
    row = lambda v: v.reshape(1, -1)
    final_gain = row(norm_final)

    w_in_b = _cast_bf16(w_in, W_IN_CAST_ROWS, "cast_w_in")
    w_out_b = _cast_bf16(w_out, W_OUT_CAST_ROWS, "cast_w_out")
    rg_w_a_b, rg_w_x_b, pool_w_b = (w.astype(_bf16) for w in (rg_w_a, rg_w_x, pool_w))

    wgu = _merge_gate_up(ffn1_gate, ffn1_up, 0)
    wd = _cast_layer_bf16(ffn1_down, 0, DOWN_CAST_ROWS, "cast_down")
    for l in range(depth):
        xf, wgu, wd = _ffn(xf, row(norm_ffn1[l]), wgu, wd, final_gain, False,
                           next_weights=(ffn2_gate, ffn2_up, ffn2_down, l))
        qkv, rest = _mix_in(xf, row(norm_mix[l]), w_in_b, l)
        att = _attention(qkv, batch, seq)
        xf = _mix_out(xf, att, rest, w_out_b, l, conv_w[l], row(conv_b[l]), rg_w_a_b[l],
                      row(rg_b_a[l]), rg_w_x_b[l], row(rg_b_x[l]), row(rg_lambda[l]),
                      pool_w_b[l], row(pool_scale[l]), batch, seq)
        if l + 1 < depth:
            xf, wgu, wd = _ffn(xf, row(norm_ffn2[l]), wgu, wd, final_gain, False,
                               next_weights=(ffn1_gate, ffn1_up, ffn1_down, l + 1))
        else:
            xf = _ffn(xf, row(norm_ffn2[l]), wgu, wd, final_gain, True)
    return xf.reshape(batch, seq, D_MODEL)
```

```python
import functools
import math

import jax
import jax.numpy as jnp
from jax import lax
from jax.experimental import pallas as pl
from jax.experimental.pallas import tpu as pltpu

D_MODEL = 2048
D_ATT = 1024
ATT_HEADS = 8
HEAD_DIM = 128
D_RNN = 512
RNN_HEADS = 4
D_POOL = 512
POOL_WINDOWS = (2, 4, 8, 16)
D_QKV = 3 * D_ATT
D_REST = 2 * D_RNN + D_POOL
D_IN_PROJ = D_QKV + D_REST
CONV_WIDTH = 4
RG_C = 8.0
D_FF = 5632
NORM_EPS = 1e-6

VMEM_LIMIT_BYTES = 60 * 1024 * 1024

FFN_TOKEN_TILE = 1024
FFN_HIDDEN_TILE = 512
MIX_TOKEN_TILE = 512
ATT_BLOCK = 256
ATT_HEADS_PER_STEP = 4
ATT_QBLOCKS_PER_STEP = 2
ATT_LOG2_CUTOFF = -100.0 * math.log2(math.e)
MASKED_LOGIT = 1e30
SEQ_TILE = 512
HISTORY_ROWS = 16
SUBLANES = 8
OUT_CHUNK = 512
DOWN_CAST_ROWS = 1408
W_IN_CAST_ROWS = 512
W_OUT_CAST_ROWS = 1024

_bf16 = jnp.bfloat16
_f32 = jnp.float32


def _rms_norm(x, gain):
    ms = jnp.mean(x * x, axis=-1, keepdims=True)
    return x * lax.rsqrt(ms + NORM_EPS) * gain


def _log_sigmoid(x):
    return jnp.minimum(x, 0.0) - jnp.log1p(jnp.exp(-jnp.abs(x)))


def _sigmoid(x):
    return 1.0 / (1.0 + jnp.exp(-x))


def _cast_kernel(w_ref, o_ref):
    o_ref[...] = w_ref[...].astype(_bf16)


def _cast_layer_bf16(w, layer, block_rows, name):
    _, rows, cols = w.shape
    return pl.pallas_call(
        _cast_kernel,
        grid=(rows // block_rows,),
        in_specs=[pl.BlockSpec((None, block_rows, cols), lambda r: (layer, r, 0))],
        out_specs=pl.BlockSpec((block_rows, cols), lambda r: (r, 0)),
        out_shape=jax.ShapeDtypeStruct((rows, cols), _bf16),
        compiler_params=pltpu.CompilerParams(
            dimension_semantics=("parallel",), vmem_limit_bytes=VMEM_LIMIT_BYTES),
        name=name,
    )(w)


def _cast_bf16(w, block_rows, name):
    depth, rows, cols = w.shape
    return pl.pallas_call(
        _cast_kernel,
        grid=(depth, rows // block_rows),
        in_specs=[pl.BlockSpec((None, block_rows, cols), lambda l, r: (l, r, 0))],
        out_specs=pl.BlockSpec((None, block_rows, cols), lambda l, r: (l, r, 0)),
        out_shape=jax.ShapeDtypeStruct(w.shape, _bf16),
        compiler_params=pltpu.CompilerParams(
            dimension_semantics=("parallel", "parallel"), vmem_limit_bytes=VMEM_LIMIT_BYTES),
        name=name,
    )(w)


def _merge_gate_up_kernel(g_ref, u_ref, o_ref):
    tf = g_ref.shape[-1]
    o_ref[:, :tf] = g_ref[...].astype(_bf16)
    o_ref[:, tf:] = u_ref[...].astype(_bf16)


def _merge_gate_up(gate, up, layer):
    tf = FFN_HIDDEN_TILE
    spec = pl.BlockSpec((None, D_MODEL, tf), lambda j: (layer, 0, j))
    return pl.pallas_call(
        _merge_gate_up_kernel,
        grid=(D_FF // tf,),
        in_specs=[spec, spec],
        out_specs=pl.BlockSpec((D_MODEL, 2 * tf), lambda j: (0, j)),
        out_shape=jax.ShapeDtypeStruct((D_MODEL, 2 * D_FF), _bf16),
        compiler_params=pltpu.CompilerParams(
            dimension_semantics=("parallel",), vmem_limit_bytes=VMEM_LIMIT_BYTES),
        name="merge_gate_up",
    )(gate, up)


def _ffn_kernel(x_ref, gain_ref, wgu_ref, wd_ref, fgain_ref, *rest, final_norm, cast_next,
                n_plain_casts):
    n_side_in = (3 if cast_next else 0) + n_plain_casts
    side_in, o_ref, side_out, h_ref = (rest[:n_side_in], rest[n_side_in],
                                       rest[n_side_in + 1:-1], rest[-1])
    if cast_next:
        _merge_gate_up_kernel(side_in[0], side_in[1], side_out[0])
        _cast_kernel(side_in[2], side_out[1])
        side_in, side_out = side_in[3:], side_out[2:]
    for w_ref, w_out_ref in zip(side_in, side_out):
        _cast_kernel(w_ref, w_out_ref)
    j = pl.program_id(1)
    tf = FFN_HIDDEN_TILE

    def hidden_tile_contribution():
        gu = jnp.dot(h_ref[...], wgu_ref[...], preferred_element_type=_f32)
        g = gu[:, :tf]
        u = gu[:, tf:]
        a = (g * _sigmoid(g) * u).astype(_bf16)
        return jnp.dot(a, wd_ref[...], preferred_element_type=_f32)

    @pl.when(j == 0)
    def _():
        h_ref[...] = _rms_norm(x_ref[...], gain_ref[...]).astype(_bf16)
        o_ref[...] = hidden_tile_contribution()

    last = pl.num_programs(1) - 1

    @pl.when(jnp.logical_and(j > 0, j < last))
    def _():
        o_ref[...] += hidden_tile_contribution()

    @pl.when(j == last)
    def _():
        if final_norm:
            o_ref[...] += hidden_tile_contribution()
            o_ref[...] = _rms_norm(x_ref[...] + 0.5 * o_ref[...], fgain_ref[...])
        else:
            o_ref[...] = x_ref[...] + 0.5 * (o_ref[...] + hidden_tile_contribution())


def _ffn(x, gain, wgu, wd, final_gain, final_norm, next_weights=None, projections=None):
    t = x.shape[0]
    tm, tf = FFN_TOKEN_TILE, FFN_HIDDEN_TILE
    n_tok = t // tm
    in_specs = [
        pl.BlockSpec((tm, D_MODEL), lambda i, j: (i, 0)),
        pl.BlockSpec((1, D_MODEL), lambda i, j: (0, 0)),
        pl.BlockSpec((D_MODEL, 2 * tf), lambda i, j: (0, j)),
        pl.BlockSpec((tf, D_MODEL), lambda i, j: (j, 0)),
        pl.BlockSpec((1, D_MODEL), lambda i, j: (0, 0)),
    ]
    out_specs = [pl.BlockSpec((tm, D_MODEL), lambda i, j: (i, 0))]
    out_shape = [jax.ShapeDtypeStruct((t, D_MODEL), _f32)]
    operands = [x, gain, wgu, wd, final_gain]
    if next_weights is not None:
        gate, up, down, nl = next_weights
        slab = D_MODEL // n_tok
        assert slab * n_tok == D_MODEL and slab % 128 == 0
        in_specs += [
            pl.BlockSpec((None, slab, tf), lambda i, j: (nl, i, j)),
            pl.BlockSpec((None, slab, tf), lambda i, j: (nl, i, j)),
            pl.BlockSpec((None, tf, slab), lambda i, j: (nl, j, i)),
        ]
        out_specs += [pl.BlockSpec((slab, 2 * tf), lambda i, j: (i, j)),
                      pl.BlockSpec((tf, slab), lambda i, j: (j, i))]
        out_shape += [jax.ShapeDtypeStruct((D_MODEL, 2 * D_FF), _bf16),
                      jax.ShapeDtypeStruct((D_FF, D_MODEL), _bf16)]
        operands += [gate, up, down]
    if projections is not None:
        w_in, in_layer, w_out = projections
        slab = D_MODEL // n_tok
        depth = w_out.shape[0]
        in_blocks = D_IN_PROJ // tf
        out_blocks = D_MODEL // tf
        assert in_blocks <= D_FF // tf and depth * out_blocks <= D_FF // tf
        in_col = lambda j: jnp.minimum(j, in_blocks - 1)
        out_idx = lambda j: jnp.minimum(j, depth * out_blocks - 1)
        in_specs += [
            pl.BlockSpec((None, slab, tf), lambda i, j: (in_layer, i, in_col(j))),
            pl.BlockSpec((None, slab, tf),
                         lambda i, j: (out_idx(j) // out_blocks, i, out_idx(j) % out_blocks)),
        ]
        out_specs += [
            pl.BlockSpec((slab, tf), lambda i, j: (i, in_col(j))),
            pl.BlockSpec((None, slab, tf),
                         lambda i, j: (out_idx(j) // out_blocks, i, out_idx(j) % out_blocks)),
        ]
        out_shape += [jax.ShapeDtypeStruct((D_MODEL, D_IN_PROJ), _bf16),
                      jax.ShapeDtypeStruct(w_out.shape, _bf16)]
        operands += [w_in, w_out]
    outs = pl.pallas_call(
        functools.partial(_ffn_kernel, final_norm=final_norm,
                          cast_next=next_weights is not None,
                          n_plain_casts=0 if projections is None else 2),
        grid=(n_tok, D_FF // tf),
        in_specs=in_specs,
        out_specs=out_specs,
        out_shape=out_shape,
        scratch_shapes=[pltpu.VMEM((tm, D_MODEL), _bf16)],
        compiler_params=pltpu.CompilerParams(
            dimension_semantics=("parallel", "arbitrary"),
            vmem_limit_bytes=VMEM_LIMIT_BYTES),
        name="ffn",
    )(*operands)
    return outs if len(outs) > 1 else outs[0]


def _mix_in_kernel(x_ref, gain_ref, w_ref, qkv_ref, rest_ref):
    h = _rms_norm(x_ref[...], gain_ref[...]).astype(_bf16)
    qkv_ref[...] = jnp.dot(h, w_ref[:, :D_QKV], preferred_element_type=_f32).astype(_bf16)
    rest_ref[...] = jnp.dot(h, w_ref[:, D_QKV:], preferred_element_type=_f32)


def _mix_in(x, gain, w_in):
    t = x.shape[0]
    tm = MIX_TOKEN_TILE
    return pl.pallas_call(
        _mix_in_kernel,
        grid=(t // tm,),
        in_specs=[
            pl.BlockSpec((tm, D_MODEL), lambda i: (i, 0)),
            pl.BlockSpec((1, D_MODEL), lambda i: (0, 0)),
            pl.BlockSpec((D_MODEL, D_IN_PROJ), lambda i: (0, 0),
                         pipeline_mode=pl.Buffered(1)),
        ],
        out_specs=[
            pl.BlockSpec((tm, D_QKV), lambda i: (i, 0)),
            pl.BlockSpec((tm, D_REST), lambda i: (i, 0)),
        ],
        out_shape=[jax.ShapeDtypeStruct((t, D_QKV), _bf16),
                   jax.ShapeDtypeStruct((t, D_REST), _f32)],
        compiler_params=pltpu.CompilerParams(
            dimension_semantics=("parallel",), vmem_limit_bytes=VMEM_LIMIT_BYTES),
        name="mix_in",
    )(x, gain, w_in)


def _log2_fail(y):
    neg_y = -y
    return jnp.minimum(neg_y, 0.0) - jnp.log2(1.0 + jnp.exp2(jnp.minimum(y, neg_y)))


def _split_bf16(x):
    hi = x.astype(_bf16)
    lo = (x - hi.astype(_f32)).astype(_bf16)
    return hi, lo


def _attention_kernel(q_ref, k_ref, v_ref, o_ref, acc_ref, carry_ref):
    i = pl.program_id(2)
    blk = ATT_BLOCK
    logit_scale = HEAD_DIM ** -0.5 * math.log2(math.e)
    row = lax.broadcasted_iota(jnp.int32, (blk, blk), 0)
    col = lax.broadcasted_iota(jnp.int32, (blk, blk), 1)
    upper = (row > col).astype(_bf16)
    contract_last = (((1,), (1,)), ((), ()))

    row2 = lax.broadcasted_iota(jnp.int32, (blk, 2 * blk), 0)
    col2 = lax.broadcasted_iota(jnp.int32, (blk, 2 * blk), 1)
    offset = col2 - row2

    chains = []
    for qb in range(ATT_QBLOCKS_PER_STEP):
        q_block = i * ATT_QBLOCKS_PER_STEP + qb
        first = jnp.maximum(q_block - 1, 0)
        start = pl.multiple_of(first * blk, blk)
        causal = offset < (q_block - first) * blk
        rows = slice(qb * blk, (qb + 1) * blk)
        for hd in range(ATT_HEADS_PER_STEP):
            lanes = slice(hd * HEAD_DIM, (hd + 1) * HEAD_DIM)
            chains.append((qb * ATT_HEADS_PER_STEP + hd, rows, lanes, first, start, causal))

    ys, lfs, css = [], [], []
    for _, rows, lanes, _, start, causal in chains:
        k2 = k_ref[pl.ds(start, 2 * blk), lanes]
        y = lax.dot_general(q_ref[rows, lanes], k2, contract_last,
                            preferred_element_type=_f32) * logit_scale
        ys.append(jnp.where(causal, y, -MASKED_LOGIT))
    for y in ys:
        lf = _log2_fail(y)
        hi, lo = _split_bf16(lf)
        stacked = jnp.concatenate([hi[:, blk:], lo[:, blk:], hi[:, :blk], lo[:, :blk]], axis=0)
        lfs.append(lf)
        css.append(jnp.dot(stacked, upper, preferred_element_type=_f32))
    for c, _, lanes, _, start, _ in chains:
        y, lf, cs = ys[c], lfs[c], css[c]
        after_r = cs[0:blk] + cs[blk:2 * blk]
        total_r = after_r[:, 0:1] + lf[:, blk:blk + 1]
        after_l = cs[2 * blk:3 * blk] + cs[3 * blk:] + total_r
        after = jnp.concatenate([after_l, after_r], axis=1)
        w = jnp.exp2(y + lf + after)
        v2 = v_ref[pl.ds(start, 2 * blk), lanes]
        acc_ref[c] = jnp.dot(w.astype(_bf16), v2, preferred_element_type=_f32)
        carry_ref[c] = after_l[:, 0:1] + lf[:, 0:1]

    def walk_earlier_blocks(c, rows, lanes, first):
        def cond(jb):
            return jnp.logical_and(jb >= 0, jnp.max(carry_ref[c]) > ATT_LOG2_CUTOFF)

        def body(jb):
            kstart = pl.multiple_of(jb * blk, blk)
            kb = k_ref[pl.ds(kstart, blk), lanes]
            vb = v_ref[pl.ds(kstart, blk), lanes]
            yb = lax.dot_general(q_ref[rows, lanes], kb, contract_last,
                                 preferred_element_type=_f32) * logit_scale
            lfb = _log2_fail(yb)
            hb, lb = _split_bf16(lfb)
            csb = jnp.dot(jnp.concatenate([hb, lb], axis=0), upper,
                          preferred_element_type=_f32)
            after_b = csb[0:blk] + csb[blk:] + carry_ref[c]
            wb = jnp.exp2(yb + lfb + after_b)
            acc_ref[c] += jnp.dot(wb.astype(_bf16), vb, preferred_element_type=_f32)
            carry_ref[c] = after_b[:, 0:1] + lfb[:, 0:1]
            return jb - 1

        lax.while_loop(cond, body, first - 1)

    hps = ATT_HEADS_PER_STEP
    any_walk = False
    for qb in range(ATT_QBLOCKS_PER_STEP):
        first = chains[qb * hps][3]
        open_mass = jnp.max(carry_ref[qb * hps:(qb + 1) * hps]) > ATT_LOG2_CUTOFF
        any_walk = jnp.logical_or(any_walk, jnp.logical_and(first >= 1, open_mass))

    @pl.when(any_walk)
    def _():
        for c, rows, lanes, first, _, _ in chains:
            walk_earlier_blocks(c, rows, lanes, first)

    for c, rows, lanes, _, _, _ in chains:
        o_ref[rows, lanes] = acc_ref[c].astype(o_ref.dtype)


def _attention(qkv, batch, seq):
    blk = ATT_BLOCK
    hps = ATT_HEADS_PER_STEP
    groups = ATT_HEADS // hps
    width = hps * HEAD_DIM
    q_rows = ATT_QBLOCKS_PER_STEP * blk
    n_chains = ATT_QBLOCKS_PER_STEP * hps
    qkv3 = qkv.reshape(batch, seq, D_QKV)
    out = pl.pallas_call(
        _attention_kernel,
        grid=(batch, groups, seq // q_rows),
        in_specs=[
            pl.BlockSpec((None, q_rows, width), lambda b, g, i: (b, i, g)),
            pl.BlockSpec((None, seq, width), lambda b, g, i: (b, 0, groups + g)),
            pl.BlockSpec((None, seq, width), lambda b, g, i: (b, 0, 2 * groups + g)),
        ],
        out_specs=pl.BlockSpec((None, q_rows, width), lambda b, g, i: (b, i, g)),
        out_shape=jax.ShapeDtypeStruct((batch, seq, D_ATT), _bf16),
        scratch_shapes=[pltpu.VMEM((n_chains, blk, HEAD_DIM), _f32),
                        pltpu.VMEM((n_chains, blk, 1), _f32)],
        compiler_params=pltpu.CompilerParams(
            dimension_semantics=("parallel", "parallel", "arbitrary"),
            vmem_limit_bytes=VMEM_LIMIT_BYTES),
        name="attention",
    )(qkv3, qkv3, qkv3)
    return out.reshape(batch * seq, D_ATT)


def _scan_rows(a, b, n):
    t_idx = lax.broadcasted_iota(jnp.int32, a.shape, 0)
    d = 1
    while d < n:
        keep = t_idx >= d
        a_sh = jnp.where(keep, pltpu.roll(a, d, 0), 1.0)
        b_sh = jnp.where(keep, pltpu.roll(b, d, 0), 0.0)
        b = a * b_sh + b
        a = a * a_sh
        d *= 2
    return a, b


def _mix_out_kernel(x_ref, att_ref, rest_ref, w_ref, conv_w_ref, conv_b_ref, wa_ref, ba_ref,
                    wx_ref, bx_ref, lam_ref, pool_w_ref, pool_scale_ref, o_ref,
                    xr_ext, xp_ext, h_ref, a_ref, b_ref, c_ref, hs_ref, rp_ref):
    s = pl.program_id(1)
    tc = SEQ_TILE
    hist = HISTORY_ROWS
    grp = SUBLANES
    n_grp = tc // grp

    @pl.when(s == 0)
    def _():
        xr_ext[0:hist, :] = jnp.zeros((hist, D_RNN), _f32)
        xp_ext[0:hist, :] = jnp.zeros((hist, D_POOL), _f32)
        h_ref[...] = jnp.zeros_like(h_ref)
        rp_ref[...] = jnp.zeros_like(rp_ref)

    def project_previous_tile(chunk):
        cols = slice(chunk * OUT_CHUNK, (chunk + 1) * OUT_CHUNK)
        y = jnp.dot(att_ref[...], w_ref[:D_ATT, cols], preferred_element_type=_f32)
        y = y + jnp.dot(rp_ref[...], w_ref[D_ATT:, cols], preferred_element_type=_f32)
        o_ref[:, cols] = x_ref[:, cols] + y

    project_previous_tile(0)
    xr_ext[hist:, :] = rest_ref[:, D_RNN:2 * D_RNN]
    xp_ext[hist:, :] = rest_ref[:, 2 * D_RNN:]

    xr_all = xr_ext[...]
    u = conv_b_ref[...] + conv_w_ref[CONV_WIDTH - 1:CONV_WIDTH, :] * xr_all[hist:, :]
    for back in range(1, CONV_WIDTH):
        j = CONV_WIDTH - 1 - back
        u = u + conv_w_ref[j:j + 1, :] * pltpu.roll(xr_all, back, 0)[hist:, :]

    project_previous_tile(1)
    ub = u.astype(_bf16)
    r_parts, i_parts = [], []
    for hd in range(RNN_HEADS):
        uh = ub[:, hd * HEAD_DIM:(hd + 1) * HEAD_DIM]
        r_parts.append(jnp.dot(uh, wa_ref[hd], preferred_element_type=_f32))
        i_parts.append(jnp.dot(uh, wx_ref[hd], preferred_element_type=_f32))
    r = _sigmoid(jnp.concatenate(r_parts, axis=1) + ba_ref[...])
    ig = _sigmoid(jnp.concatenate(i_parts, axis=1) + bx_ref[...])
    log_a = RG_C * r * _log_sigmoid(lam_ref[...])
    a = jnp.exp(log_a)
    var = -jnp.tanh(log_a) * (1.0 + a * a)
    b = jnp.where(var > 0.0, var * lax.rsqrt(var), 0.0) * (ig * u)

    project_previous_tile(2)
    a3 = a.reshape(n_grp, grp, D_RNN)
    b3 = b.reshape(n_grp, grp, D_RNN)
    sub = lax.broadcasted_iota(jnp.int32, (n_grp, grp, D_RNN), 1)
    d = 1
    while d < grp:
        keep = sub >= d
        a_sh = jnp.where(keep, pltpu.roll(a3, d, 1), 1.0)
        b_sh = jnp.where(keep, pltpu.roll(b3, d, 1), 0.0)
        b3 = a3 * b_sh + b3
        a3 = a3 * a_sh
        d *= 2
    a = a3.reshape(tc, D_RNN)
    b = b3.reshape(tc, D_RNN)
    project_previous_tile(3)
    g_idx = lax.broadcasted_iota(jnp.int32, (n_grp, HEAD_DIM), 0)
    group_ends = pl.ds(grp - 1, n_grp, stride=grp)
    for hd in range(RNN_HEADS):
        lanes = slice(hd * HEAD_DIM, (hd + 1) * HEAD_DIM)
        a_ref[hd] = a[:, lanes]
        b_ref[hd] = b[:, lanes]
        a_end, b_end = _scan_rows(a_ref[hd, group_ends, :], b_ref[hd, group_ends, :], n_grp)
        h_in = h_ref[:, lanes]
        h_end = a_end * h_in + b_end
        c_ref[hd] = jnp.where(g_idx == 0, h_in, pltpu.roll(h_end, 1, 0))
        h_ref[:, lanes] = h_end[n_grp - 1:n_grp, :]
        for k in range(n_grp):
            rows = slice(k * grp, (k + 1) * grp)
            hs_ref[rows, lanes] = (a_ref[hd, rows, :] * c_ref[hd, k:k + 1, :]
                                   + b_ref[hd, rows, :])

    xg = rest_ref[:, 0:D_RNN]
    c0 = math.sqrt(2.0 / math.pi)
    gelu = 0.5 * xg * (1.0 + jnp.tanh(c0 * (xg + 0.044715 * (xg * xg * xg))))
    rp_ref[:, 0:D_RNN] = (gelu * hs_ref[...]).astype(_bf16)

    tile = jnp.minimum(s, pl.num_programs(1) - 2)
    pos = tile * tc + lax.broadcasted_iota(jnp.int32, (tc, HEAD_DIM), 0)
    for g, win in enumerate(POOL_WINDOWS):
        lo, hi = g * HEAD_DIM, (g + 1) * HEAD_DIM
        tot = xp_ext[:, lo:hi]
        span = 1
        while span < win:
            tot = tot + pltpu.roll(tot, span, 0)
            span *= 2
        cur = xp_ext[hist:, lo:hi]
        count = jnp.minimum(pos + 1, win).astype(_f32)
        dlt = (tot[hist:, :] / count - cur).astype(_bf16)
        y = jnp.dot(dlt, pool_w_ref[g], preferred_element_type=_f32)
        rp_ref[:, D_RNN + lo:D_RNN + hi] = (y * pool_scale_ref[:, lo:hi]).astype(_bf16)

    xr_ext[0:hist, :] = xr_ext[tc:tc + hist, :]
    xp_ext[0:hist, :] = xp_ext[tc:tc + hist, :]


def _mix_out(x, att, rest, w_out, layer, conv_w, conv_b, w_a, b_a, w_x, b_x, lam, pool_w,
             pool_scale, batch, seq):
    tc = SEQ_TILE
    n_grp = tc // SUBLANES
    n_tiles = seq // tc
    lagged = lambda width: pl.BlockSpec(
        (None, tc, width), lambda b, s: (b, jnp.maximum(s - 1, 0), 0))
    current = lambda width: pl.BlockSpec(
        (None, tc, width), lambda b, s: (b, jnp.minimum(s, n_tiles - 1), 0))
    vec = lambda n: pl.BlockSpec((1, n), lambda b, s: (0, 0))
    mats = pl.BlockSpec((RNN_HEADS, HEAD_DIM, HEAD_DIM), lambda b, s: (0, 0, 0))
    out = pl.pallas_call(
        _mix_out_kernel,
        grid=(batch, n_tiles + 1),
        in_specs=[
            lagged(D_MODEL), lagged(D_ATT), current(D_REST),
            pl.BlockSpec((None, D_MODEL, D_MODEL), lambda b, s: (layer, 0, 0),
                         pipeline_mode=pl.Buffered(1)),
            pl.BlockSpec((CONV_WIDTH, D_RNN), lambda b, s: (0, 0)),
            vec(D_RNN), mats, vec(D_RNN), mats, vec(D_RNN), vec(D_RNN),
            mats, vec(D_POOL),
        ],
        out_specs=lagged(D_MODEL),
        out_shape=jax.ShapeDtypeStruct((batch, seq, D_MODEL), _f32),
        scratch_shapes=[
            pltpu.VMEM((HISTORY_ROWS + tc, D_RNN), _f32),
            pltpu.VMEM((HISTORY_ROWS + tc, D_POOL), _f32),
            pltpu.VMEM((1, D_RNN), _f32),
            pltpu.VMEM((RNN_HEADS, tc, HEAD_DIM), _f32),
            pltpu.VMEM((RNN_HEADS, tc, HEAD_DIM), _f32),
            pltpu.VMEM((RNN_HEADS, n_grp, HEAD_DIM), _f32),
            pltpu.VMEM((tc, D_RNN), _f32),
            pltpu.VMEM((tc, D_RNN + D_POOL), _bf16),
        ],
        compiler_params=pltpu.CompilerParams(
            dimension_semantics=("parallel", "arbitrary"),
            vmem_limit_bytes=VMEM_LIMIT_BYTES),
        name="mix_out",
    )(x.reshape(batch, seq, D_MODEL), att.reshape(batch, seq, D_ATT),
      rest.reshape(batch, seq, D_REST), w_out, conv_w, conv_b, w_a, b_a, w_x, b_x, lam,
      pool_w, pool_scale)
    return out.reshape(batch * seq, D_MODEL)


def kernel(x, norm_ffn1, ffn1_gate, ffn1_up, ffn1_down, norm_mix, w_in, conv_w, conv_b, rg_w_a, rg_b_a, rg_w_x, rg_b_x, rg_lambda, pool_w, pool_scale, w_out, norm_ffn2, ffn2_gate, ffn2_up, ffn2_down, norm_final):
    batch, seq, _ = x.shape
    depth = w_in.shape[0]
    assert x.shape == (batch, seq, D_MODEL) and seq % max(2 * ATT_BLOCK, SEQ_TILE) == 0
    assert (batch * seq) % max(FFN_TOKEN_TILE, MIX_TOKEN_TILE) == 0
    xf = x.reshape(batch * seq, D_MODEL)
    row = lambda v: v.reshape(1, -1)
    final_gain = row(norm_final)

    rg_w_a_b, rg_w_x_b, pool_w_b = (w.astype(_bf16) for w in (rg_w_a, rg_w_x, pool_w))

    wgu = _merge_gate_up(ffn1_gate, ffn1_up, 0)
    wd = _cast_layer_bf16(ffn1_down, 0, DOWN_CAST_ROWS, "cast_down")
    w_in_b = _cast_layer_bf16(w_in, 0, W_IN_CAST_ROWS, "cast_w_in")
    w_out_b = _cast_bf16(w_out, W_OUT_CAST_ROWS, "cast_w_out") if depth == 1 else None
    for l in range(depth):
        ffn2_weights = (ffn2_gate, ffn2_up, ffn2_down, l)
        if l + 1 < depth:
            xf, wgu, wd, w_in_next, w_out_b = _ffn(
                xf, row(norm_ffn1[l]), wgu, wd, final_gain, False, next_weights=ffn2_weights,
                projections=(w_in, l + 1, w_out))
        else:
            xf, wgu, wd = _ffn(xf, row(norm_ffn1[l]), wgu, wd, final_gain, False,
                               next_weights=ffn2_weights)
            w_in_next = None
        qkv, rest = _mix_in(xf, row(norm_mix[l]), w_in_b)
        w_in_b = w_in_next
        att = _attention(qkv, batch, seq)
        xf = _mix_out(xf, att, rest, w_out_b, l, conv_w[l], row(conv_b[l]), rg_w_a_b[l],
                      row(rg_b_a[l]), rg_w_x_b[l], row(rg_b_x[l]), row(rg_lambda[l]),
                      pool_w_b[l], row(pool_scale[l]), batch, seq)
        if l + 1 < depth:
            xf, wgu, wd = _ffn(xf, row(norm_ffn2[l]), wgu, wd, final_gain, False,
                               next_weights=(ffn1_gate, ffn1_up, ffn1_down, l + 1))
        else:
            xf = _ffn(xf, row(norm_ffn2[l]), wgu, wd, final_gain, True)
    return xf.reshape(batch, seq, D_MODEL)
```
